```python
import math
import jax, jax.numpy as jnp
from jax import lax
import numpy as np

D_MODEL = 1024
BATCH = 32
SEQ = 2048
DEPTH = 1

MOBA_HEADS = 8
MOBA_HEAD_DIM = 64
MOBA_BLOCK = 256
MOBA_TOPK = 3
MOBA_Q_CHUNK = 16
RET_HEADS = 8
RET_QK_DIM = 64
RET_V_DIM = 128
RET_CHUNK = 128
D_FF = 4 * D_MODEL
ROPE_THETA = 10000.0
EPS = 1e-6

MOBA_W = MOBA_HEADS * MOBA_HEAD_DIM
RET_QK_W = RET_HEADS * RET_QK_DIM
RET_V_W = RET_HEADS * RET_V_DIM
IN_SPLITS = (MOBA_W, MOBA_W, MOBA_W, RET_QK_W, RET_QK_W, RET_V_W, RET_V_W, D_MODEL, D_MODEL)
IN_WIDTH = 3 * MOBA_W + 2 * RET_QK_W + 2 * RET_V_W + 2 * D_MODEL
N_ADA = 6

kernel_name = "hybrid_moba_retention_adaln_block"


def rms_norm(x, w):
    xf = x.astype(jnp.float32)
    y = xf * lax.rsqrt(jnp.mean(xf * xf, axis=-1, keepdims=True) + EPS)
    return (y * w.astype(jnp.float32)).astype(x.dtype)


def modulate(h, shift, scale):
    return h * (1 + scale) + shift


def rotary_tables(seq, inv_freq):
    pos = jnp.arange(seq, dtype=jnp.float32)
    ang = pos[:, None] * inv_freq[None, :]
    return jnp.cos(ang), jnp.sin(ang)


def apply_rotary(t, cos, sin):
    t1, t2 = jnp.split(t, 2, axis=-1)
    c = cos.astype(t.dtype)
    s = sin.astype(t.dtype)
    return jnp.concatenate([t1 * c - t2 * s, t1 * s + t2 * c], axis=-1)


def to_heads(t, n_heads):
    b, s, _ = t.shape
    return t.reshape(b, s, n_heads, -1).transpose(0, 2, 1, 3)


def from_heads(t):
    b, h, s, d = t.shape
    return t.transpose(0, 2, 1, 3).reshape(b, s, h * d)


def moba_attention(q, k, v):
    B, H, S, dh = q.shape
    T = MOBA_BLOCK
    Qc = MOBA_Q_CHUNK
    nb = -(-S // T)
    pad = nb * T - S
    kp = jnp.pad(k, ((0, 0), (0, 0), (0, pad), (0, 0)))
    vp = jnp.pad(v, ((0, 0), (0, 0), (0, pad), (0, 0)))
    kb = kp.reshape(B, H, nb, T, dh)
    vb = vp.reshape(B, H, nb, T, dh)
    scale = dh ** -0.5
    n_sel = min(MOBA_TOPK, nb - 1)
    nq = S // Qc
    q_c = q.reshape(B, H, nq, Qc, dh).transpose(2, 0, 1, 3, 4)

    if n_sel > 0:
        q_blk = jnp.arange(S) // T
        k_mean = jnp.mean(kb.astype(jnp.float32), axis=3)
        gate = jnp.einsum('bhsd,bhnd->bhsn', q.astype(jnp.float32), k_mean)
        past = jnp.arange(nb)[None, :] < q_blk[:, None]
        gate = jnp.where(past[None, None], gate, -jnp.inf)
        sel_score, sel_idx = lax.top_k(gate, n_sel)
        sel_valid = jnp.isfinite(sel_score)
        idx_c = sel_idx.reshape(B, H, nq, Qc, n_sel).transpose(2, 0, 1, 3, 4)
        valid_c = sel_valid.reshape(B, H, nq, Qc, n_sel).transpose(2, 0, 1, 3, 4)
    else:
        idx_c = jnp.zeros((nq, B, H, Qc, 1), jnp.int32)
        valid_c = jnp.zeros((nq, B, H, Qc, 1), bool)

    gather_blocks = jax.vmap(jax.vmap(lambda blocks, idx: blocks[idx]))

    def one_chunk(args):
        ci, q_i, idx_i, valid_i = args
        start = ci * Qc
        blk_start = (start // T) * T
        k_own = lax.dynamic_slice_in_dim(kp, blk_start, T, axis=2)
        v_own = lax.dynamic_slice_in_dim(vp, blk_start, T, axis=2)
        qpos = start + jnp.arange(Qc)
        kpos = blk_start + jnp.arange(T)
        s_own = jnp.einsum('bhqd,bhtd->bhqt', q_i, k_own).astype(jnp.float32) * scale
        s_own = jnp.where(kpos[None, :] <= qpos[:, None], s_own, -jnp.inf)
        if n_sel == 0:
            p_own = jax.nn.softmax(s_own, axis=-1).astype(v.dtype)
            return jnp.einsum('bhqt,bhtd->bhqd', p_own, v_own)
        k_sel = gather_blocks(kb, idx_i)
        v_sel = gather_blocks(vb, idx_i)
        s_sel = jnp.einsum('bhqd,bhqntd->bhqnt', q_i, k_sel).astype(jnp.float32) * scale
        s_sel = jnp.where(valid_i[..., None], s_sel, -jnp.inf)
        s_all = jnp.concatenate([s_sel.reshape(B, H, Qc, n_sel * T), s_own], axis=-1)
        p = jax.nn.softmax(s_all, axis=-1).astype(v.dtype)
        p_sel = p[..., :n_sel * T].reshape(B, H, Qc, n_sel, T)
        p_own = p[..., n_sel * T:]
        return (jnp.einsum('bhqnt,bhqntd->bhqd', p_sel, v_sel)
                + jnp.einsum('bhqt,bhtd->bhqd', p_own, v_own))

    out = lax.map(one_chunk, (jnp.arange(nq), q_c, idx_c, valid_c))
    return out.transpose(1, 2, 0, 3, 4).reshape(B, H, S, dh)


def retention_chunkwise(q, k, v):
    B, H, S, dk = q.shape
    dv = v.shape[-1]
    C = RET_CHUNK
    nc = S // C
    f32 = jnp.float32
    q = q.astype(f32)
    k = k.astype(f32) * (dk ** -0.5)
    v = v.astype(f32)
    log_g = jnp.log(1.0 - jnp.power(2.0, -5.0 - jnp.arange(H, dtype=f32)))
    qc = q.reshape(B, H, nc, C, dk)
    kc = k.reshape(B, H, nc, C, dk)
    vc = v.reshape(B, H, nc, C, dv)
    i = jnp.arange(C, dtype=f32)
    diff = i[:, None] - i[None, :]
    decay = jnp.where(diff >= 0, jnp.exp(jnp.maximum(diff, 0.0)[None] * log_g[:, None, None]), 0.0)
    scores = jnp.einsum('bhnid,bhnjd->bhnij', qc, kc) * decay[None, :, None]
    o_intra = jnp.einsum('bhnij,bhnje->bhnie', scores, vc)
    zeta = jnp.exp((C - 1 - i)[None, :] * log_g[:, None])
    kv = jnp.einsum('bhnjd,bhnje->bhnde', kc * zeta[None, :, None, :, None], vc)
    chunk_decay = jnp.exp(C * log_g)[None, :, None, None]

    def step(state, kv_n):
        return state * chunk_decay + kv_n, state

    _, r_prev = lax.scan(step, jnp.zeros((B, H, dk, dv), f32), kv.transpose(2, 0, 1, 3, 4))
    r_prev = r_prev.transpose(1, 2, 0, 3, 4)
    xi = jnp.exp((i + 1)[None, :] * log_g[:, None])
    o_cross = jnp.einsum('bhnid,bhnde->bhnie', qc, r_prev) * xi[None, :, None, :, None]
    return (o_intra + o_cross).reshape(B, H, S, dv)


def head_group_norm(o, gain):
    mu = jnp.mean(o, axis=-1, keepdims=True)
    var = jnp.mean(jnp.square(o - mu), axis=-1, keepdims=True)
    y = (o - mu) * lax.rsqrt(var + EPS)
    return from_heads(y) * gain.astype(jnp.float32)


def setup_inputs(seed: int = 0) -> dict:
    key = jax.random.key(seed)
    ks = jax.random.split(key, 16)
    f32 = jnp.float32

    def nrm(k, shape, fan_in):
        return jax.random.normal(k, shape, f32) * (fan_in ** -0.5)

    return {
        "x": jax.random.normal(ks[0], (BATCH, SEQ, D_MODEL), f32),
        "c": jax.random.normal(ks[1], (BATCH, D_MODEL), f32),
        "ln1_w": 1.0 + 0.02 * jax.random.normal(ks[2], (DEPTH, D_MODEL), f32),
        "ln2_w": 1.0 + 0.02 * jax.random.normal(ks[3], (DEPTH, D_MODEL), f32),
        "w_ada": nrm(ks[4], (DEPTH, D_MODEL, N_ADA * D_MODEL), D_MODEL),
        "b_ada": 0.02 * jax.random.normal(ks[5], (DEPTH, N_ADA * D_MODEL), f32),
        "w_in": nrm(ks[6], (DEPTH, D_MODEL, IN_WIDTH), D_MODEL),
        "ret_gn_w": 1.0 + 0.02 * jax.random.normal(ks[7], (DEPTH, RET_V_W), f32),
        "w_moba_o": nrm(ks[8], (DEPTH, MOBA_W, D_MODEL), MOBA_W),
        "w_ret_o": nrm(ks[9], (DEPTH, RET_V_W, D_MODEL), RET_V_W),
        "w_out": nrm(ks[10], (DEPTH, D_MODEL, D_MODEL), D_MODEL),
        "w_ff1": nrm(ks[11], (DEPTH, D_MODEL, D_FF), D_MODEL),
        "w_ff2": nrm(ks[12], (DEPTH, D_FF, D_MODEL), D_FF),
        "final_norm_w": 1.0 + 0.02 * jax.random.normal(ks[13], (D_MODEL,), f32),
    }


def reference(x, c, ln1_w, ln2_w, w_ada, b_ada, w_in, ret_gn_w, w_moba_o, w_ret_o,
              w_out, w_ff1, w_ff2, final_norm_w):
    S = x.shape[1]
    rope_inv = 1.0 / (ROPE_THETA ** (jnp.arange(0, MOBA_HEAD_DIM, 2, dtype=jnp.float32) / MOBA_HEAD_DIM))
    moba_cos, moba_sin = rotary_tables(S, rope_inv)
    ret_inv = 1.0 / (ROPE_THETA ** jnp.linspace(0.0, 1.0, RET_QK_DIM // 2, dtype=jnp.float32))
    ret_cos, ret_sin = rotary_tables(S, ret_inv)
    split_at = [int(v) for v in np.cumsum(IN_SPLITS)[:-1]]
    c_act = jax.nn.silu(c)

    for l in range(DEPTH):
        ada = (c_act @ w_ada[l] + b_ada[l])[:, None, :]
        sh1, sc1, g1, sh2, sc2, g2 = jnp.split(ada, N_ADA, axis=-1)

        h = modulate(rms_norm(x, ln1_w[l]), sh1, sc1)
        proj = h @ w_in[l]
        mq, mk, mv, rq, rk, rv, rg, ga, gr = jnp.split(proj, split_at, axis=-1)

        mq = apply_rotary(to_heads(mq, MOBA_HEADS), moba_cos, moba_sin)
        mk = apply_rotary(to_heads(mk, MOBA_HEADS), moba_cos, moba_sin)
        a_out = moba_attention(mq, mk, to_heads(mv, MOBA_HEADS))
        y_a = from_heads(a_out) @ w_moba_o[l]

        rq = apply_rotary(to_heads(rq, RET_HEADS), ret_cos, ret_sin)
        rk = apply_rotary(to_heads(rk, RET_HEADS), ret_cos, ret_sin)
        r_out = retention_chunkwise(rq, rk, to_heads(rv, RET_HEADS))
        r_out = head_group_norm(r_out, ret_gn_w[l]).astype(x.dtype)
        y_r = (jax.nn.silu(rg) * r_out) @ w_ret_o[l]

        merged = jax.nn.sigmoid(ga) * y_a + jax.nn.sigmoid(gr) * y_r
        x = x + g1 * (merged @ w_out[l])

        h2 = modulate(rms_norm(x, ln2_w[l]), sh2, sc2)
        x = x + g2 * (jnp.square(jax.nn.relu(h2 @ w_ff1[l])) @ w_ff2[l])

    return rms_norm(x, final_norm_w)
```

```python
import functools
import math

import jax
import jax.numpy as jnp
from jax import lax
from jax.experimental import pallas as pl
from jax.experimental.pallas import tpu as pltpu

F32 = jnp.float32
BF16 = jnp.bfloat16

D_MODEL = 1024
N_ADA = 6
HEAD_DIM = 64
HEADS = 8
PAIR_W = 2 * HEAD_DIM
N_PAIRS = HEADS // 2
MOBA_W = HEADS * HEAD_DIM
MOBA_BLOCK = 256
MOBA_TOPK = 3
RET_V_DIM = 128
RET_V_W = HEADS * RET_V_DIM
RET_CHUNK = 256
D_FF = 4 * D_MODEL
FF_CHUNK = 1024
ROPE_THETA = 10000.0
EPS = 1e-6
QKV_W = 3 * MOBA_W + 2 * MOBA_W + RET_V_W
GATES_W = RET_V_W + 2 * D_MODEL
IN_WIDTH = QKV_W + GATES_W
PROJ_CHUNK = 512
TOKEN_TILE = 512
VMEM_LIMIT = 60 * 1024 * 1024

NT_DIMS = (((1,), (1,)), ((), ()))


def _sigmoid(t):
    return 1.0 / (1.0 + jnp.exp(-t))


def _rms_norm(x, w):
    return x * lax.rsqrt(jnp.mean(x * x, axis=-1, keepdims=True) + EPS) * w


def _ada_kernel(c_ref, w_ref, b_ref, o_ref):
    c = c_ref[...]
    ca = c * _sigmoid(c)
    o_ref[...] = jnp.dot(ca, w_ref[...], precision=lax.Precision.HIGHEST,
                         preferred_element_type=F32) + b_ref[...]


def _ada(c, w_ada, b_ada):
    B, D = c.shape
    N = w_ada.shape[1]
    bn = 1024
    return pl.pallas_call(
        _ada_kernel,
        grid=(N // bn,),
        in_specs=[pl.BlockSpec((B, D), lambda j: (0, 0)),
                  pl.BlockSpec((D, bn), lambda j: (0, j)),
                  pl.BlockSpec((1, bn), lambda j: (0, j))],
        out_specs=pl.BlockSpec((B, bn), lambda j: (0, j)),
        out_shape=jax.ShapeDtypeStruct((B, N), F32),
        name="ada",
    )(c, w_ada, b_ada.reshape(1, N))


def _rotary(tk, tab_ref, which):
    return (tk * tab_ref[which, 0]
            + pltpu.roll(tk, HEAD_DIM // 2, 1) * tab_ref[which, 1]
            + pltpu.roll(tk, PAIR_W - HEAD_DIM // 2, 1) * tab_ref[which, 2])


def _in_proj_kernel(x_ref, ada_ref, ln_ref, tab_ref, w_ref, qkv_ref, gates_ref):
    x = x_ref[0]
    shift = ada_ref[0, 0:1, :]
    scale = ada_ref[0, 1:2, :]
    h = (_rms_norm(x, ln_ref[...]) * (1.0 + scale) + shift).astype(BF16)
    rot = {0: 0, 1: 1, 3: 2, 4: 3}
    for ci in range(IN_WIDTH // PROJ_CHUNK):
        c0 = ci * PROJ_CHUNK
        t = jnp.dot(h, w_ref[:, c0:c0 + PROJ_CHUNK], preferred_element_type=F32)
        if ci in rot:
            for i in range(PROJ_CHUNK // PAIR_W):
                lo = i * PAIR_W
                piece = _rotary(t[:, lo:lo + PAIR_W], tab_ref, rot[ci])
                qkv_ref[0, :, c0 + lo:c0 + lo + PAIR_W] = piece.astype(BF16)
        elif c0 < QKV_W:
            qkv_ref[0, :, c0:c0 + PROJ_CHUNK] = t.astype(BF16)
        else:
            gates_ref[0, :, c0 - QKV_W:c0 - QKV_W + PROJ_CHUNK] = t.astype(BF16)


def _in_proj(x, ada3, ln1_w, tabs, w_in_bf16):
    B, S, D = x.shape
    tm = TOKEN_TILE
    return pl.pallas_call(
        _in_proj_kernel,
        grid=(S // tm, B),
        in_specs=[pl.BlockSpec((1, tm, D), lambda si, b: (b, si, 0)),
                  pl.BlockSpec((1, N_ADA, D), lambda si, b: (b, 0, 0)),
                  pl.BlockSpec((1, D), lambda si, b: (0, 0)),
                  pl.BlockSpec((4, 3, tm, PAIR_W), lambda si, b: (0, 0, si, 0)),
                  pl.BlockSpec((D, IN_WIDTH), lambda si, b: (0, 0),
                               pipeline_mode=pl.Buffered(1))],
        out_specs=[pl.BlockSpec((1, tm, QKV_W), lambda si, b: (b, si, 0)),
                   pl.BlockSpec((1, tm, GATES_W), lambda si, b: (b, si, 0))],
        out_shape=[jax.ShapeDtypeStruct((B, S, QKV_W), BF16),
                   jax.ShapeDtypeStruct((B, S, GATES_W), BF16)],
        compiler_params=pltpu.CompilerParams(vmem_limit_bytes=VMEM_LIMIT),
        name="in_proj",
    )(x, ada3, ln1_w.reshape(1, D), tabs, w_in_bf16)


def _moba_kernel(q_ref, k_ref, v_ref, o_ref, kh_scr, vt_scr, bias_scr, acc_scr):
    S = q_ref.shape[1]
    T = MOBA_BLOCK
    nb = S // T
    lane = lax.broadcasted_iota(jnp.int32, (1, PAIR_W), 1)
    qf = q_ref[0].astype(F32)
    k = k_ref[0]
    for n in range(nb):
        vt_scr[n] = v_ref[0, n * T:(n + 1) * T, :].T

    blk = lax.broadcasted_iota(jnp.int32, (nb, S), 0)
    qblk = lax.broadcasted_iota(jnp.int32, (nb, S), 1) // T
    past = blk < qblk
    krow = lax.broadcasted_iota(jnp.int32, (T, T), 0)
    qcol = lax.broadcasted_iota(jnp.int32, (T, T), 1)
    causal = krow <= qcol

    for h in range(2):
        hm = (lane >= h * HEAD_DIM) & (lane < (h + 1) * HEAD_DIM)
        kh = jnp.where(hm, k, jnp.zeros_like(k))
        kh_scr[h] = kh
        kmean = jnp.mean(kh.astype(F32).reshape(nb, T, PAIR_W), axis=1)
        gate = lax.dot_general(kmean, qf, NT_DIMS, precision=lax.Precision.HIGHEST,
                               preferred_element_type=F32)
        g = jnp.where(past, gate, -jnp.inf)
        rank = jnp.zeros((nb, S), jnp.int32)
        for m in range(nb):
            gm = g[m:m + 1, :]
            beats = (gm > g) | ((gm == g) & (m < blk))
            rank = rank + beats.astype(jnp.int32)
        sel = past & (rank < MOBA_TOPK)
        bias = jnp.where(sel, 0.0, -jnp.inf).astype(F32)
        for j in range(nb):
            bias_scr[h, j] = bias[:, j * T:(j + 1) * T]

    for h in range(2):
        r_lo = h * HEAD_DIM

        def q_tile(j, carry, h=h, r_lo=r_lo):
            r0 = pl.multiple_of(j * T, T)
            qt = q_ref[0, pl.ds(r0, T), :]
            s = lax.dot_general(kh_scr[h, pl.ds(r0, T), :], qt, NT_DIMS,
                                preferred_element_type=F32)
            s = jnp.where(causal, s, -jnp.inf)
            m = jnp.max(s, axis=0, keepdims=True)
            p = jnp.exp(s - m)
            l = jnp.sum(p, axis=0, keepdims=True)
            acc = jnp.dot(vt_scr[j, r_lo:r_lo + HEAD_DIM, :], p.astype(BF16),
                          preferred_element_type=F32)

            def past_block(n, mla):
                m, l, acc = mla
                c0 = pl.multiple_of(n * T, T)
                s = lax.dot_general(kh_scr[h, pl.ds(c0, T), :], qt, NT_DIMS,
                                    preferred_element_type=F32)
                s = s + bias_scr[h, j, pl.ds(n, 1), :]
                m_new = jnp.maximum(m, jnp.max(s, axis=0, keepdims=True))
                alpha = jnp.exp(m - m_new)
                p = jnp.exp(s - m_new)
                l = alpha * l + jnp.sum(p, axis=0, keepdims=True)
                acc = alpha * acc + jnp.dot(vt_scr[n, r_lo:r_lo + HEAD_DIM, :], p.astype(BF16),
                                            preferred_element_type=F32)
                return m_new, l, acc

            m, l, acc = lax.fori_loop(0, j, past_block, (m, l, acc))
            acc_scr[j, r_lo:r_lo + HEAD_DIM, :] = acc / l
            return carry

        lax.fori_loop(0, nb, q_tile, 0)

    for j in range(nb):
        o_ref[0, j * T:(j + 1) * T, :] = acc_scr[j].T.astype(BF16)


def _moba(qkv):
    B, S, _ = qkv.shape
    nb = S // MOBA_BLOCK
    k_off = MOBA_W // PAIR_W
    v_off = 2 * MOBA_W // PAIR_W
    return pl.pallas_call(
        _moba_kernel,
        grid=(B, N_PAIRS),
        in_specs=[pl.BlockSpec((1, S, PAIR_W), lambda b, p: (b, 0, p)),
                  pl.BlockSpec((1, S, PAIR_W), lambda b, p: (b, 0, k_off + p)),
                  pl.BlockSpec((1, S, PAIR_W), lambda b, p: (b, 0, v_off + p))],
        out_specs=pl.BlockSpec((1, S, PAIR_W), lambda b, p: (b, 0, p)),
        out_shape=jax.ShapeDtypeStruct((B, S, MOBA_W), BF16),
        scratch_shapes=[pltpu.VMEM((2, S, PAIR_W), BF16),
                        pltpu.VMEM((nb, PAIR_W, MOBA_BLOCK), BF16),
                        pltpu.VMEM((2, nb, nb, MOBA_BLOCK), F32),
                        pltpu.VMEM((nb, PAIR_W, MOBA_BLOCK), F32)],
        name="moba",
    )(qkv, qkv, qkv)


def _ret_kernel(q_ref, k_ref, v_ref, g_ref, gn_ref, dec_ref, zeta_ref, xi_ref, cdec_ref,
                o_ref, kt_scr):
    S = q_ref.shape[1]
    C = RET_CHUNK
    nc = S // C
    lane = lax.broadcasted_iota(jnp.int32, (1, PAIR_W), 1)
    for n in range(nc):
        kt_scr[n] = k_ref[0, n * C:(n + 1) * C, :].T

    for h in range(2):
        hm = (lane >= h * HEAD_DIM) & (lane < (h + 1) * HEAD_DIM)
        v_lo = h * RET_V_DIM

        def chunk(n, st, h=h, hm=hm, v_lo=v_lo):
            r0 = pl.multiple_of(n * C, C)
            q = q_ref[0, pl.ds(r0, C), :]
            qh = jnp.where(hm, q, jnp.zeros_like(q))
            kc = k_ref[0, pl.ds(r0, C), :]
            vh = v_ref[0, pl.ds(r0, C), v_lo:v_lo + RET_V_DIM]
            s = lax.dot_general(qh, kc, NT_DIMS, preferred_element_type=F32)
            p = (s * dec_ref[h]).astype(BF16)
            o = jnp.dot(p, vh, preferred_element_type=F32)
            o = o + jnp.dot(qh, st.astype(BF16), preferred_element_type=F32) * xi_ref[h]
            kz = (kt_scr[n].astype(F32) * zeta_ref[h]).astype(BF16)
            st = st * cdec_ref[h] + jnp.dot(kz, vh, preferred_element_type=F32)
            mu = jnp.mean(o, axis=-1, keepdims=True)
            d = o - mu
            var = jnp.mean(d * d, axis=-1, keepdims=True)
            y = d * lax.rsqrt(var + EPS) * gn_ref[:, v_lo:v_lo + RET_V_DIM]
            g = g_ref[0, pl.ds(r0, C), v_lo:v_lo + RET_V_DIM].astype(F32)
            o_ref[0, pl.ds(r0, C), v_lo:v_lo + RET_V_DIM] = (g * _sigmoid(g) * y).astype(BF16)
            return st

        lax.fori_loop(0, nc, chunk, jnp.zeros((PAIR_W, RET_V_DIM), F32))


def _ret_tables():
    C = RET_CHUNK
    log_g = jnp.log(1.0 - jnp.power(2.0, -5.0 - jnp.arange(HEADS, dtype=F32)))
    i = jnp.arange(C, dtype=F32)
    diff = i[:, None] - i[None, :]
    dec = jnp.where(diff >= 0, jnp.exp(jnp.maximum(diff, 0.0)[None] * log_g[:, None, None]), 0.0)
    zeta = jnp.exp((C - 1 - i)[None, :] * log_g[:, None])[:, None, :]
    xi = jnp.exp((i + 1)[None, :] * log_g[:, None])
    xi = jnp.broadcast_to(xi[:, :, None], (HEADS, C, RET_V_DIM))
    cdec = jnp.broadcast_to(jnp.exp(C * log_g)[:, None, None], (HEADS, 1, RET_V_DIM))
    return dec, zeta, xi, cdec


def _ret(qkv, gates, gn_w):
    B, S, _ = qkv.shape
    C = RET_CHUNK
    q_off = 3 * MOBA_W // PAIR_W
    k_off = 4 * MOBA_W // PAIR_W
    v_off = 5 * MOBA_W // (2 * RET_V_DIM)
    dec, zeta, xi, cdec = _ret_tables()
    return pl.pallas_call(
        _ret_kernel,
        grid=(B, N_PAIRS),
        in_specs=[pl.BlockSpec((1, S, PAIR_W), lambda b, p: (b, 0, q_off + p)),
                  pl.BlockSpec((1, S, PAIR_W), lambda b, p: (b, 0, k_off + p)),
                  pl.BlockSpec((1, S, 2 * RET_V_DIM), lambda b, p: (b, 0, v_off + p)),
                  pl.BlockSpec((1, S, 2 * RET_V_DIM), lambda b, p: (b, 0, p)),
                  pl.BlockSpec((1, 2 * RET_V_DIM), lambda b, p: (0, p)),
                  pl.BlockSpec((2, C, C), lambda b, p: (p, 0, 0)),
                  pl.BlockSpec((2, 1, C), lambda b, p: (p, 0, 0)),
                  pl.BlockSpec((2, C, RET_V_DIM), lambda b, p: (p, 0, 0)),
                  pl.BlockSpec((2, 1, RET_V_DIM), lambda b, p: (p, 0, 0))],
        out_specs=pl.BlockSpec((1, S, 2 * RET_V_DIM), lambda b, p: (b, 0, p)),
        out_shape=jax.ShapeDtypeStruct((B, S, RET_V_W), BF16),
        scratch_shapes=[pltpu.VMEM((S // C, PAIR_W, C), BF16)],
        name="ret",
    )(qkv, qkv, qkv, gates, gn_w.reshape(1, RET_V_W), dec, zeta, xi, cdec)


def _out_mlp_kernel(x_ref, a_ref, r_ref, ga_ref, gr_ref, ada_ref, ln2_ref, fw_ref,
                    wmo_ref, wro_ref, wout_ref, w1_ref, w2_ref, o_ref):
    x = x_ref[0]
    ya = jnp.dot(a_ref[0], wmo_ref[...], preferred_element_type=F32)
    yr = jnp.dot(r_ref[0], wro_ref[...], preferred_element_type=F32)
    merged = (_sigmoid(ga_ref[0].astype(F32)) * ya
              + _sigmoid(gr_ref[0].astype(F32)) * yr)
    mo = jnp.dot(merged.astype(BF16), wout_ref[...], preferred_element_type=F32)
    gate1 = ada_ref[0, 2:3, :]
    shift2 = ada_ref[0, 3:4, :]
    scale2 = ada_ref[0, 4:5, :]
    gate2 = ada_ref[0, 5:6, :]
    x1 = x + gate1 * mo
    h2 = (_rms_norm(x1, ln2_ref[...]) * (1.0 + scale2) + shift2).astype(BF16)
    acc = jnp.zeros_like(x1)
    for ci in range(D_FF // FF_CHUNK):
        c0 = ci * FF_CHUNK
        hid = jnp.dot(h2, w1_ref[:, c0:c0 + FF_CHUNK], preferred_element_type=F32)
        hid = jnp.square(jnp.maximum(hid, 0.0)).astype(BF16)
        acc = acc + jnp.dot(hid, w2_ref[c0:c0 + FF_CHUNK, :], preferred_element_type=F32)
    x2 = x1 + gate2 * acc
    o_ref[0] = _rms_norm(x2, fw_ref[...])


def _out_mlp(x, a_out, r_out, gates, ada3, ln2_w, final_w, wmo, wro, wout, w1, w2):
    B, S, D = x.shape
    tm = TOKEN_TILE
    const = functools.partial(pl.BlockSpec, pipeline_mode=pl.Buffered(1))
    return pl.pallas_call(
        _out_mlp_kernel,
        grid=(B, S // tm),
        in_specs=[pl.BlockSpec((1, tm, D), lambda b, si: (b, si, 0)),
                  pl.BlockSpec((1, tm, MOBA_W), lambda b, si: (b, si, 0)),
                  pl.BlockSpec((1, tm, RET_V_W), lambda b, si: (b, si, 0)),
                  pl.BlockSpec((1, tm, D), lambda b, si: (b, si, 1)),
                  pl.BlockSpec((1, tm, D), lambda b, si: (b, si, 2)),
                  pl.BlockSpec((1, N_ADA, D), lambda b, si: (b, 0, 0)),
                  pl.BlockSpec((1, D), lambda b, si: (0, 0)),
                  pl.BlockSpec((1, D), lambda b, si: (0, 0)),
                  const((MOBA_W, D), lambda b, si: (0, 0)),
                  const((RET_V_W, D), lambda b, si: (0, 0)),
                  const((D, D), lambda b, si: (0, 0)),
                  const((D, D_FF), lambda b, si: (0, 0)),
                  const((D_FF, D), lambda b, si: (0, 0))],
        out_specs=pl.BlockSpec((1, tm, D), lambda b, si: (b, si, 0)),
        out_shape=jax.ShapeDtypeStruct((B, S, D), F32),
        compiler_params=pltpu.CompilerParams(vmem_limit_bytes=VMEM_LIMIT),
        name="out_mlp",
    )(x, a_out, r_out, gates, gates, ada3, ln2_w.reshape(1, D), final_w.reshape(1, D),
      wmo, wro, wout, w1, w2)


def _rotary_tables(S, inv_freq, scale):
    pos = jnp.arange(S, dtype=F32)
    ang = pos[:, None] * inv_freq[None, :]
    cos, sin = jnp.cos(ang), jnp.sin(ang)
    zero = jnp.zeros_like(sin)
    reps = PAIR_W // HEAD_DIM
    cos_t = jnp.tile(jnp.concatenate([cos, cos], axis=1), (1, reps))
    sin_up = jnp.tile(jnp.concatenate([zero, sin], axis=1), (1, reps))
    sin_lo = jnp.tile(jnp.concatenate([-sin, zero], axis=1), (1, reps))
    return jnp.stack([cos_t, sin_up, sin_lo]) * scale


def kernel(x, c, ln1_w, ln2_w, w_ada, b_ada, w_in, ret_gn_w, w_moba_o, w_ret_o,
           w_out, w_ff1, w_ff2, final_norm_w):
    B, S, D = x.shape
    assert w_in.shape[0] == 1, "final norm is fused into the single layer's MLP kernel"
    moba_inv = 1.0 / (ROPE_THETA ** (jnp.arange(0, HEAD_DIM, 2, dtype=F32) / HEAD_DIM))
    ret_inv = 1.0 / (ROPE_THETA ** jnp.linspace(0.0, 1.0, HEAD_DIM // 2, dtype=F32))
    qk_scale = HEAD_DIM ** -0.5
    tabs = jnp.stack([_rotary_tables(S, moba_inv, qk_scale),
                      _rotary_tables(S, moba_inv, 1.0),
                      _rotary_tables(S, ret_inv, 1.0),
                      _rotary_tables(S, ret_inv, qk_scale)])

    ada3 = _ada(c, w_ada[0], b_ada[0]).reshape(B, N_ADA, D)
    qkv, gates = _in_proj(x, ada3, ln1_w[0], tabs, w_in[0].astype(BF16))
    a_out = _moba(qkv)
    r_out = _ret(qkv, gates, ret_gn_w[0])
    return _out_mlp(x, a_out, r_out, gates, ada3, ln2_w[0], final_norm_w,
                    w_moba_o[0].astype(BF16), w_ret_o[0].astype(BF16), w_out[0].astype(BF16),
                    w_ff1[0].astype(BF16), w_ff2[0].astype(BF16))
```

```python
import functools
import math

import jax
import jax.numpy as jnp
from jax import lax
from jax.experimental import pallas as pl
from jax.experimental.pallas import tpu as pltpu

F32 = jnp.float32
BF16 = jnp.bfloat16

D_MODEL = 1024
N_ADA = 6
HEAD_DIM = 64
HEADS = 8
PAIR_W = 2 * HEAD_DIM
N_PAIRS = HEADS // 2
MOBA_W = HEADS * HEAD_DIM
MOBA_BLOCK = 256
MOBA_TOPK = 3
RET_V_DIM = 128
RET_V_W = HEADS * RET_V_DIM
RET_CHUNK = 256
D_FF = 4 * D_MODEL
FF_CHUNK = 1024
ROPE_THETA = 10000.0
EPS = 1e-6
LOG2E = math.log2(math.e)
QKV_W = 3 * MOBA_W + 2 * MOBA_W + RET_V_W
GATES_W = RET_V_W + 2 * D_MODEL
IN_WIDTH = QKV_W + GATES_W
PROJ_CHUNK = 512
TOKEN_TILE = 512
VMEM_LIMIT = 60 * 1024 * 1024

NT_DIMS = (((1,), (1,)), ((), ()))


def _sigmoid(t):
    return 1.0 / (1.0 + jnp.exp(-t))


def _rms_norm(x, w):
    return x * lax.rsqrt(jnp.mean(x * x, axis=-1, keepdims=True) + EPS) * w


def _ada_kernel(c_ref, w_ref, b_ref, o_ref):
    c = c_ref[...]
    ca = c * _sigmoid(c)
    o_ref[...] = jnp.dot(ca, w_ref[...], precision=lax.Precision.HIGHEST,
                         preferred_element_type=F32) + b_ref[...]


def _ada(c, w_ada, b_ada):
    B, D = c.shape
    N = w_ada.shape[1]
    bn = 1024
    return pl.pallas_call(
        _ada_kernel,
        grid=(N // bn,),
        in_specs=[pl.BlockSpec((B, D), lambda j: (0, 0)),
                  pl.BlockSpec((D, bn), lambda j: (0, j)),
                  pl.BlockSpec((1, bn), lambda j: (0, j))],
        out_specs=pl.BlockSpec((B, bn), lambda j: (0, j)),
        out_shape=jax.ShapeDtypeStruct((B, N), F32),
        name="ada",
    )(c, w_ada, b_ada.reshape(1, N))


def _rotary(tk, tab_ref, which):
    return (tk * tab_ref[which, 0]
            + pltpu.roll(tk, HEAD_DIM // 2, 1) * tab_ref[which, 1]
            + pltpu.roll(tk, PAIR_W - HEAD_DIM // 2, 1) * tab_ref[which, 2])


def _in_proj_kernel(x_ref, ada_ref, ln_ref, tab_ref, w_ref, qkv_ref, gates_ref):
    x = x_ref[0]
    shift = ada_ref[0, 0:1, :]
    scale = ada_ref[0, 1:2, :]
    h = (_rms_norm(x, ln_ref[...]) * (1.0 + scale) + shift).astype(BF16)
    rot = {0: 0, 1: 1, 3: 2, 4: 3}
    for ci in range(IN_WIDTH // PROJ_CHUNK):
        c0 = ci * PROJ_CHUNK
        t = jnp.dot(h, w_ref[:, c0:c0 + PROJ_CHUNK], preferred_element_type=F32)
        if ci in rot:
            for i in range(PROJ_CHUNK // PAIR_W):
                lo = i * PAIR_W
                piece = _rotary(t[:, lo:lo + PAIR_W], tab_ref, rot[ci])
                qkv_ref[0, :, c0 + lo:c0 + lo + PAIR_W] = piece.astype(BF16)
        elif c0 < QKV_W:
            qkv_ref[0, :, c0:c0 + PROJ_CHUNK] = t.astype(BF16)
        else:
            gates_ref[0, :, c0 - QKV_W:c0 - QKV_W + PROJ_CHUNK] = t.astype(BF16)


def _in_proj(x, ada3, ln1_w, tabs, w_in_bf16):
    B, S, D = x.shape
    tm = TOKEN_TILE
    return pl.pallas_call(
        _in_proj_kernel,
        grid=(S // tm, B),
        in_specs=[pl.BlockSpec((1, tm, D), lambda si, b: (b, si, 0)),
                  pl.BlockSpec((1, N_ADA, D), lambda si, b: (b, 0, 0)),
                  pl.BlockSpec((1, D), lambda si, b: (0, 0)),
                  pl.BlockSpec((4, 3, tm, PAIR_W), lambda si, b: (0, 0, si, 0)),
                  pl.BlockSpec((D, IN_WIDTH), lambda si, b: (0, 0),
                               pipeline_mode=pl.Buffered(1))],
        out_specs=[pl.BlockSpec((1, tm, QKV_W), lambda si, b: (b, si, 0)),
                   pl.BlockSpec((1, tm, GATES_W), lambda si, b: (b, si, 0))],
        out_shape=[jax.ShapeDtypeStruct((B, S, QKV_W), BF16),
                   jax.ShapeDtypeStruct((B, S, GATES_W), BF16)],
        compiler_params=pltpu.CompilerParams(vmem_limit_bytes=VMEM_LIMIT),
        name="in_proj",
    )(x, ada3, ln1_w.reshape(1, D), tabs, w_in_bf16)


def _moba_kernel(q_ref, k_ref, v_ref, o_ref, kh_scr, vt_scr):
    S = q_ref.shape[1]
    T = MOBA_BLOCK
    nb = S // T
    lane = lax.broadcasted_iota(jnp.int32, (1, PAIR_W), 1)
    k = k_ref[0]
    for n in range(nb):
        vt_scr[n] = v_ref[0, n * T:(n + 1) * T, :].T

    kmean = jnp.mean(k.astype(F32).reshape(nb, T, PAIR_W), axis=1)
    km_rows = []
    for h in range(2):
        hm = (lane >= h * HEAD_DIM) & (lane < (h + 1) * HEAD_DIM)
        kh_scr[h] = jnp.where(hm, k, jnp.zeros_like(k))
        kmh = jnp.where(hm, kmean, 0.0)
        hi = kmh.astype(BF16).astype(F32)
        mid = (kmh - hi).astype(BF16).astype(F32)
        km_rows += [hi, mid, kmh - hi - mid]
    km = jnp.concatenate(km_rows, axis=0).astype(BF16)
    gate_terms = lax.dot_general(km, q_ref[0], NT_DIMS, preferred_element_type=F32)

    blk = lax.broadcasted_iota(jnp.int32, (nb, S), 0)
    qblk = lax.broadcasted_iota(jnp.int32, (nb, S), 1) // T
    past = blk < qblk
    biases = []
    for h in range(2):
        gt = gate_terms[3 * nb * h:3 * nb * (h + 1)]
        gate = gt[0:nb] + gt[nb:2 * nb] + gt[2 * nb:3 * nb]
        g = jnp.where(past, gate, -jnp.inf)
        rank = jnp.zeros((nb, S), jnp.int32)
        for m in range(nb - 1):
            gm = g[m:m + 1, :]
            beats = (gm > g) | ((gm == g) & (m < blk))
            rank = rank + beats.astype(jnp.int32)
        sel = past & (rank < MOBA_TOPK)
        biases.append(jnp.where(sel, 0.0, -jnp.inf).astype(F32))

    krow = lax.broadcasted_iota(jnp.int32, (T, T), 0)
    qcol = lax.broadcasted_iota(jnp.int32, (T, T), 1)
    causal = krow <= qcol

    units = [(j, h) for j in range(nb) for h in range(2)]

    def score_steps(j, h, st):
        qt = q_ref[0, j * T:(j + 1) * T, :]
        st["blocks"], st["m"] = [], None
        for n in range(j + 1):
            def step(n=n):
                sn = lax.dot_general(kh_scr[h, n * T:(n + 1) * T, :], qt, NT_DIMS,
                                     preferred_element_type=F32)
                if n == j:
                    sn = jnp.where(causal, sn, -jnp.inf)
                    cm = jnp.max(sn, axis=0, keepdims=True)
                else:
                    cm = jnp.max(sn, axis=0, keepdims=True) + biases[h][n:n + 1, j * T:(j + 1) * T]
                st["blocks"].append(sn)
                st["m"] = cm if st["m"] is None else jnp.maximum(st["m"], cm)
            yield step

    def value_steps(j, h, st, outs):
        r_lo = h * HEAD_DIM
        acc = {"l": jnp.zeros((1, T), F32), "o": jnp.zeros((HEAD_DIM, T), F32)}
        for n in range(j + 1):
            def step(n=n):
                m = st["m"]
                off = m if n == j else m - biases[h][n:n + 1, j * T:(j + 1) * T]
                p = jnp.exp2(st["blocks"][n] - off)
                acc["l"] = acc["l"] + jnp.sum(p, axis=0, keepdims=True)
                acc["o"] = acc["o"] + jnp.dot(vt_scr[n, r_lo:r_lo + HEAD_DIM, :], p.astype(BF16),
                                              preferred_element_type=F32)
            yield step

        def finish():
            outs.append(acc["o"] * (1.0 / acc["l"]))
            if h == 1:
                o_ref[0, j * T:(j + 1) * T, :] = jnp.concatenate(outs, axis=0).T.astype(BF16)
        yield finish

    states = [dict() for _ in units]
    for step in score_steps(*units[0], states[0]):
        step()
    outs = []
    for i, (j, h) in enumerate(units):
        if h == 0:
            outs = []
        nxt = list(score_steps(*units[i + 1], states[i + 1])) if i + 1 < len(units) else []
        cur = list(value_steps(j, h, states[i], outs))
        for t in range(max(len(nxt), len(cur))):
            if t < len(nxt):
                nxt[t]()
            if t < len(cur):
                cur[t]()
        states[i].clear()


def _moba(qkv):
    B, S, _ = qkv.shape
    nb = S // MOBA_BLOCK
    k_off = MOBA_W // PAIR_W
    v_off = 2 * MOBA_W // PAIR_W
    return pl.pallas_call(
        _moba_kernel,
        grid=(B, N_PAIRS),
        in_specs=[pl.BlockSpec((1, S, PAIR_W), lambda b, p: (b, 0, p)),
                  pl.BlockSpec((1, S, PAIR_W), lambda b, p: (b, 0, k_off + p)),
                  pl.BlockSpec((1, S, PAIR_W), lambda b, p: (b, 0, v_off + p))],
        out_specs=pl.BlockSpec((1, S, PAIR_W), lambda b, p: (b, 0, p)),
        out_shape=jax.ShapeDtypeStruct((B, S, MOBA_W), BF16),
        scratch_shapes=[pltpu.VMEM((2, S, PAIR_W), BF16),
                        pltpu.VMEM((nb, PAIR_W, MOBA_BLOCK), BF16)],
        name="moba",
    )(qkv, qkv, qkv)


def _ret_kernel(q_ref, k_ref, v_ref, g_ref, gn_ref, dec_ref, zeta_ref, xi_ref, cdec_ref, o_ref):
    S = q_ref.shape[1]
    C = RET_CHUNK
    nc = S // C
    lane = lax.broadcasted_iota(jnp.int32, (1, PAIR_W), 1)
    state = [None, None]
    for n in range(nc):
        rows = slice(n * C, (n + 1) * C)
        q = q_ref[0, rows, :]
        kc = k_ref[0, rows, :]
        kt = kc.T.astype(F32)
        for h in range(2):
            hm = (lane >= h * HEAD_DIM) & (lane < (h + 1) * HEAD_DIM)
            cols = slice(h * RET_V_DIM, (h + 1) * RET_V_DIM)
            qh = jnp.where(hm, q, jnp.zeros_like(q))
            vh = v_ref[0, rows, cols]
            s = lax.dot_general(qh, kc, NT_DIMS, preferred_element_type=F32)
            p = (s * dec_ref[h]).astype(BF16)
            o = jnp.dot(p, vh, preferred_element_type=F32)
            if n > 0:
                o = o + jnp.dot(qh, state[h].astype(BF16), preferred_element_type=F32) * xi_ref[h]
            if n < nc - 1:
                kz = (kt * zeta_ref[h]).astype(BF16)
                kv = jnp.dot(kz, vh, preferred_element_type=F32)
                state[h] = kv if n == 0 else state[h] * cdec_ref[h] + kv
            mu = jnp.mean(o, axis=-1, keepdims=True)
            d = o - mu
            var = jnp.mean(d * d, axis=-1, keepdims=True)
            y = d * lax.rsqrt(var + EPS) * gn_ref[:, cols]
            g = g_ref[0, rows, cols].astype(F32)
            o_ref[0, rows, cols] = (g * _sigmoid(g) * y).astype(BF16)


def _ret_tables():
    C = RET_CHUNK
    log_g = jnp.log(1.0 - jnp.power(2.0, -5.0 - jnp.arange(HEADS, dtype=F32)))
    i = jnp.arange(C, dtype=F32)
    diff = i[:, None] - i[None, :]
    dec = jnp.where(diff >= 0, jnp.exp(jnp.maximum(diff, 0.0)[None] * log_g[:, None, None]), 0.0)
    zeta = jnp.exp((C - 1 - i)[None, :] * log_g[:, None])[:, None, :]
    xi = jnp.exp((i + 1)[None, :] * log_g[:, None])
    xi = jnp.broadcast_to(xi[:, :, None], (HEADS, C, RET_V_DIM))
    cdec = jnp.broadcast_to(jnp.exp(C * log_g)[:, None, None], (HEADS, 1, RET_V_DIM))
    return dec, zeta, xi, cdec


def _ret(qkv, gates, gn_w):
    B, S, _ = qkv.shape
    C = RET_CHUNK
    q_off = 3 * MOBA_W // PAIR_W
    k_off = 4 * MOBA_W // PAIR_W
    v_off = 5 * MOBA_W // (2 * RET_V_DIM)
    dec, zeta, xi, cdec = _ret_tables()
    return pl.pallas_call(
        _ret_kernel,
        grid=(B, N_PAIRS),
        in_specs=[pl.BlockSpec((1, S, PAIR_W), lambda b, p: (b, 0, q_off + p)),
                  pl.BlockSpec((1, S, PAIR_W), lambda b, p: (b, 0, k_off + p)),
                  pl.BlockSpec((1, S, 2 * RET_V_DIM), lambda b, p: (b, 0, v_off + p)),
                  pl.BlockSpec((1, S, 2 * RET_V_DIM), lambda b, p: (b, 0, p)),
                  pl.BlockSpec((1, 2 * RET_V_DIM), lambda b, p: (0, p)),
                  pl.BlockSpec((2, C, C), lambda b, p: (p, 0, 0)),
                  pl.BlockSpec((2, 1, C), lambda b, p: (p, 0, 0)),
                  pl.BlockSpec((2, C, RET_V_DIM), lambda b, p: (p, 0, 0)),
                  pl.BlockSpec((2, 1, RET_V_DIM), lambda b, p: (p, 0, 0))],
        out_specs=pl.BlockSpec((1, S, 2 * RET_V_DIM), lambda b, p: (b, 0, p)),
        out_shape=jax.ShapeDtypeStruct((B, S, RET_V_W), BF16),
        name="ret",
    )(qkv, qkv, qkv, gates, gn_w.reshape(1, RET_V_W), dec, zeta, xi, cdec)


def _out_mlp_kernel(x_ref, a_ref, r_ref, ga_ref, gr_ref, ada_ref, ln2_ref, fw_ref,
                    wmo_ref, wro_ref, wout_ref, w1_ref, w2_ref, o_ref):
    x = x_ref[0]
    ya = jnp.dot(a_ref[0], wmo_ref[...], preferred_element_type=F32)
    yr = jnp.dot(r_ref[0], wro_ref[...], preferred_element_type=F32)
    merged = (_sigmoid(ga_ref[0].astype(F32)) * ya
              + _sigmoid(gr_ref[0].astype(F32)) * yr)
    mo = jnp.dot(merged.astype(BF16), wout_ref[...], preferred_element_type=F32)
    gate1 = ada_ref[0, 2:3, :]
    shift2 = ada_ref[0, 3:4, :]
    scale2 = ada_ref[0, 4:5, :]
    gate2 = ada_ref[0, 5:6, :]
    x1 = x + gate1 * mo
    h2 = (_rms_norm(x1, ln2_ref[...]) * (1.0 + scale2) + shift2).astype(BF16)
    acc = jnp.zeros_like(x1)
    for ci in range(D_FF // FF_CHUNK):
        c0 = ci * FF_CHUNK
        hid = jnp.dot(h2, w1_ref[:, c0:c0 + FF_CHUNK], preferred_element_type=F32)
        hid = jnp.square(jnp.maximum(hid, 0.0)).astype(BF16)
        acc = acc + jnp.dot(hid, w2_ref[c0:c0 + FF_CHUNK, :], preferred_element_type=F32)
    x2 = x1 + gate2 * acc
    o_ref[0] = _rms_norm(x2, fw_ref[...])


def _out_mlp(x, a_out, r_out, gates, ada3, ln2_w, final_w, wmo, wro, wout, w1, w2):
    B, S, D = x.shape
    tm = TOKEN_TILE
    const = functools.partial(pl.BlockSpec, pipeline_mode=pl.Buffered(1))
    return pl.pallas_call(
        _out_mlp_kernel,
        grid=(B, S // tm),
        in_specs=[pl.BlockSpec((1, tm, D), lambda b, si: (b, si, 0)),
                  pl.BlockSpec((1, tm, MOBA_W), lambda b, si: (b, si, 0)),
                  pl.BlockSpec((1, tm, RET_V_W), lambda b, si: (b, si, 0)),
                  pl.BlockSpec((1, tm, D), lambda b, si: (b, si, 1)),
                  pl.BlockSpec((1, tm, D), lambda b, si: (b, si, 2)),
                  pl.BlockSpec((1, N_ADA, D), lambda b, si: (b, 0, 0)),
                  pl.BlockSpec((1, D), lambda b, si: (0, 0)),
                  pl.BlockSpec((1, D), lambda b, si: (0, 0)),
                  const((MOBA_W, D), lambda b, si: (0, 0)),
                  const((RET_V_W, D), lambda b, si: (0, 0)),
                  const((D, D), lambda b, si: (0, 0)),
                  const((D, D_FF), lambda b, si: (0, 0)),
                  const((D_FF, D), lambda b, si: (0, 0))],
        out_specs=pl.BlockSpec((1, tm, D), lambda b, si: (b, si, 0)),
        out_shape=jax.ShapeDtypeStruct((B, S, D), F32),
        compiler_params=pltpu.CompilerParams(vmem_limit_bytes=VMEM_LIMIT),
        name="out_mlp",
    )(x, a_out, r_out, gates, gates, ada3, ln2_w.reshape(1, D), final_w.reshape(1, D),
      wmo, wro, wout, w1, w2)


def _rotary_tables(S, inv_freq, scale):
    pos = jnp.arange(S, dtype=F32)
    ang = pos[:, None] * inv_freq[None, :]
    cos, sin = jnp.cos(ang), jnp.sin(ang)
    zero = jnp.zeros_like(sin)
    reps = PAIR_W // HEAD_DIM
    cos_t = jnp.tile(jnp.concatenate([cos, cos], axis=1), (1, reps))
    sin_up = jnp.tile(jnp.concatenate([zero, sin], axis=1), (1, reps))
    sin_lo = jnp.tile(jnp.concatenate([-sin, zero], axis=1), (1, reps))
    return jnp.stack([cos_t, sin_up, sin_lo]) * scale


def kernel(x, c, ln1_w, ln2_w, w_ada, b_ada, w_in, ret_gn_w, w_moba_o, w_ret_o,
           w_out, w_ff1, w_ff2, final_norm_w):
    B, S, D = x.shape
    assert w_in.shape[0] == 1, "final norm is fused into the single layer's MLP kernel"
    moba_inv = 1.0 / (ROPE_THETA ** (jnp.arange(0, HEAD_DIM, 2, dtype=F32) / HEAD_DIM))
    ret_inv = 1.0 / (ROPE_THETA ** jnp.linspace(0.0, 1.0, HEAD_DIM // 2, dtype=F32))
    qk_scale = HEAD_DIM ** -0.5
    tabs = jnp.stack([_rotary_tables(S, moba_inv, qk_scale * LOG2E),
                      _rotary_tables(S, moba_inv, 1.0),
                      _rotary_tables(S, ret_inv, 1.0),
                      _rotary_tables(S, ret_inv, qk_scale)])

    ada3 = _ada(c, w_ada[0], b_ada[0]).reshape(B, N_ADA, D)
    qkv, gates = _in_proj(x, ada3, ln1_w[0], tabs, w_in[0].astype(BF16))
    a_out = _moba(qkv)
    r_out = _ret(qkv, gates, ret_gn_w[0])
    return _out_mlp(x, a_out, r_out, gates, ada3, ln2_w[0], final_norm_w,
                    w_moba_o[0].astype(BF16), w_ret_o[0].astype(BF16), w_out[0].astype(BF16),
                    w_ff1[0].astype(BF16), w_ff2[0].astype(BF16))
```

```python
import functools
import math

import jax
import jax.numpy as jnp
from jax import lax
from jax.experimental import pallas as pl
from jax.experimental.pallas import tpu as pltpu

F32 = jnp.float32
BF16 = jnp.bfloat16

D_MODEL = 1024
N_ADA = 6
HEAD_DIM = 64
HEADS = 8
PAIR_W = 2 * HEAD_DIM
N_PAIRS = HEADS // 2
MOBA_W = HEADS * HEAD_DIM
MOBA_BLOCK = 256
MOBA_TOPK = 3
SUM_ROWS = 16
RET_V_DIM = 128
RET_V_W = HEADS * RET_V_DIM
RET_CHUNK = 256
D_FF = 4 * D_MODEL
FF_CHUNK = 1024
ROPE_THETA = 10000.0
EPS = 1e-6
LOG2E = math.log2(math.e)
QKV_W = 3 * MOBA_W + 2 * MOBA_W + RET_V_W
GATES_W = RET_V_W + 2 * D_MODEL
IN_WIDTH = QKV_W + GATES_W
PROJ_CHUNK = 512
TOKEN_TILE = 512
VMEM_LIMIT = 60 * 1024 * 1024

NT_DIMS = (((1,), (1,)), ((), ()))


def _sigmoid(t):
    return 1.0 / (1.0 + jnp.exp(-t))


def _rms_norm(x, w):
    return x * lax.rsqrt(jnp.mean(x * x, axis=-1, keepdims=True) + EPS) * w


def _ada_kernel(c_ref, w_ref, b_ref, o_ref):
    c = c_ref[...]
    ca = c * _sigmoid(c)
    o_ref[...] = jnp.dot(ca, w_ref[...], precision=lax.Precision.HIGHEST,
                         preferred_element_type=F32) + b_ref[...]


def _ada(c, w_ada, b_ada):
    B, D = c.shape
    N = w_ada.shape[1]
    bn = 1024
    return pl.pallas_call(
        _ada_kernel,
        grid=(N // bn,),
        in_specs=[pl.BlockSpec((B, D), lambda j: (0, 0)),
                  pl.BlockSpec((D, bn), lambda j: (0, j)),
                  pl.BlockSpec((1, bn), lambda j: (0, j))],
        out_specs=pl.BlockSpec((B, bn), lambda j: (0, j)),
        out_shape=jax.ShapeDtypeStruct((B, N), F32),
        name="ada",
    )(c, w_ada, b_ada.reshape(1, N))


def _rotary(tk, tab_ref, which):
    return (tk * tab_ref[which, 0]
            + pltpu.roll(tk, HEAD_DIM // 2, 1) * tab_ref[which, 1]
            + pltpu.roll(tk, PAIR_W - HEAD_DIM // 2, 1) * tab_ref[which, 2])


def _in_proj_kernel(x_ref, ada_ref, ln_ref, tab_ref, w_ref, qkv_ref, gates_ref):
    x = x_ref[0]
    shift = ada_ref[0, 0:1, :]
    scale = ada_ref[0, 1:2, :]
    h = (_rms_norm(x, ln_ref[...]) * (1.0 + scale) + shift).astype(BF16)
    rot = {0: 0, 1: 1, 3: 2, 4: 3}
    for ci in range(IN_WIDTH // PROJ_CHUNK):
        c0 = ci * PROJ_CHUNK
        t = jnp.dot(h, w_ref[:, c0:c0 + PROJ_CHUNK], preferred_element_type=F32)
        if ci in rot:
            for i in range(PROJ_CHUNK // PAIR_W):
                lo = i * PAIR_W
                piece = _rotary(t[:, lo:lo + PAIR_W], tab_ref, rot[ci])
                qkv_ref[0, :, c0 + lo:c0 + lo + PAIR_W] = piece.astype(BF16)
        elif c0 < QKV_W:
            qkv_ref[0, :, c0:c0 + PROJ_CHUNK] = t.astype(BF16)
        else:
            gates_ref[0, :, c0 - QKV_W:c0 - QKV_W + PROJ_CHUNK] = t.astype(BF16)


def _in_proj(x, ada3, ln1_w, tabs, w_in_bf16):
    B, S, D = x.shape
    tm = TOKEN_TILE
    return pl.pallas_call(
        _in_proj_kernel,
        grid=(S // tm, B),
        in_specs=[pl.BlockSpec((1, tm, D), lambda si, b: (b, si, 0)),
                  pl.BlockSpec((1, N_ADA, D), lambda si, b: (b, 0, 0)),
                  pl.BlockSpec((1, D), lambda si, b: (0, 0)),
                  pl.BlockSpec((4, 3, tm, PAIR_W), lambda si, b: (0, 0, si, 0)),
                  pl.BlockSpec((D, IN_WIDTH), lambda si, b: (0, 0),
                               pipeline_mode=pl.Buffered(1))],
        out_specs=[pl.BlockSpec((1, tm, QKV_W), lambda si, b: (b, si, 0)),
                   pl.BlockSpec((1, tm, GATES_W), lambda si, b: (b, si, 0))],
        out_shape=[jax.ShapeDtypeStruct((B, S, QKV_W), BF16),
                   jax.ShapeDtypeStruct((B, S, GATES_W), BF16)],
        compiler_params=pltpu.CompilerParams(vmem_limit_bytes=VMEM_LIMIT),
        name="in_proj",
    )(x, ada3, ln1_w.reshape(1, D), tabs, w_in_bf16)


def _moba_kernel(q_ref, k_ref, v_ref, o_ref, vt_scr):
    S = q_ref.shape[1]
    T = MOBA_BLOCK
    nb = S // T
    lane = lax.broadcasted_iota(jnp.int32, (1, PAIR_W), 1)
    head_masks = [(lane >= h * HEAD_DIM) & (lane < (h + 1) * HEAD_DIM) for h in range(2)]
    krow = lax.broadcasted_iota(jnp.int32, (T, 2 * T), 0)
    qcol = lax.broadcasted_iota(jnp.int32, (T, 2 * T), 1) % T
    causal2 = krow <= qcol

    def score_steps(j, st):
        qt = q_ref[0, j * T:(j + 1) * T, :]
        zero = jnp.zeros_like(qt)
        q2 = jnp.concatenate([jnp.where(hm, qt, zero) for hm in head_masks], axis=0)
        st["blocks"], st["cmax"] = [], []
        for n in range(j + 1):
            def step(n=n):
                sn = lax.dot_general(k_ref[0, n * T:(n + 1) * T, :], q2, NT_DIMS,
                                     preferred_element_type=F32)
                if n == j:
                    sn = jnp.where(causal2, sn, -jnp.inf)
                st["blocks"].append(sn)
                st["cmax"].append(jnp.max(sn, axis=0, keepdims=True))
            yield step

    order = list(range(nb - 1, -1, -1))
    states = [dict() for _ in range(nb)]
    for step in score_steps(order[0], states[order[0]]):
        step()

    for n in range(nb):
        vt = v_ref[0, n * T:(n + 1) * T, :].T
        for h in range(2):
            vt_scr[h, n, 0:HEAD_DIM, :] = vt[h * HEAD_DIM:(h + 1) * HEAD_DIM]
            vt_scr[h, n, HEAD_DIM:, :] = jnp.ones((SUM_ROWS, T), BF16)

    kmean = jnp.mean(k_ref[0].astype(F32).reshape(nb, T, PAIR_W), axis=1)
    km_rows = []
    for hm in head_masks:
        kmh = jnp.where(hm, kmean, 0.0)
        hi = kmh.astype(BF16).astype(F32)
        mid = (kmh - hi).astype(BF16).astype(F32)
        km_rows += [hi, mid, kmh - hi - mid]
    km = jnp.concatenate(km_rows, axis=0).astype(BF16)
    gate_terms = lax.dot_general(km, q_ref[0], NT_DIMS, preferred_element_type=F32)

    blk = lax.broadcasted_iota(jnp.int32, (nb, S), 0)
    qblk = lax.broadcasted_iota(jnp.int32, (nb, S), 1) // T
    past = blk < qblk
    biases = []
    for h in range(2):
        gt = gate_terms[3 * nb * h:3 * nb * (h + 1)]
        gate = gt[0:nb] + gt[nb:2 * nb] + gt[2 * nb:3 * nb]
        g = jnp.where(past, gate, -jnp.inf)
        rank = jnp.zeros((nb, S), jnp.int32)
        for m in range(nb - 1):
            gm = g[m:m + 1, :]
            beats = (gm > g) | ((gm == g) & (m < blk))
            rank = rank + beats.astype(jnp.int32)
        sel = past & (rank < MOBA_TOPK)
        biases.append(jnp.where(sel, 0.0, -jnp.inf).astype(F32))

    def value_steps(j, st):
        bias = jnp.concatenate([b[:, j * T:(j + 1) * T] for b in biases], axis=1)
        m = st["cmax"][j]
        for n in range(j):
            m = jnp.maximum(m, st["cmax"][n] + bias[n:n + 1])
        acc = [jnp.zeros((HEAD_DIM + SUM_ROWS, T), F32) for _ in range(2)]
        for n in range(j + 1):
            def step(n=n):
                off = m if n == j else m - bias[n:n + 1]
                p = jnp.exp2((st["blocks"][n] - off).astype(BF16))
                for h in range(2):
                    acc[h] = acc[h] + jnp.dot(vt_scr[h, n], p[:, h * T:(h + 1) * T],
                                              preferred_element_type=F32)
            yield step

        def finish():
            outs = [a[0:HEAD_DIM] * (1.0 / a[HEAD_DIM:HEAD_DIM + 1]) for a in acc]
            o_ref[0, j * T:(j + 1) * T, :] = jnp.concatenate(outs, axis=0).T.astype(BF16)
        yield finish

    for i, j in enumerate(order):
        nxt = list(score_steps(order[i + 1], states[order[i + 1]])) if i + 1 < nb else []
        cur = list(value_steps(j, states[j]))
        for t in range(max(len(nxt), len(cur))):
            if t < len(nxt):
                nxt[t]()
            if t < len(cur):
                cur[t]()
        states[j].clear()


def _moba(qkv):
    B, S, _ = qkv.shape
    nb = S // MOBA_BLOCK
    k_off = MOBA_W // PAIR_W
    v_off = 2 * MOBA_W // PAIR_W
    return pl.pallas_call(
        _moba_kernel,
        grid=(B, N_PAIRS),
        in_specs=[pl.BlockSpec((1, S, PAIR_W), lambda b, p: (b, 0, p)),
                  pl.BlockSpec((1, S, PAIR_W), lambda b, p: (b, 0, k_off + p)),
                  pl.BlockSpec((1, S, PAIR_W), lambda b, p: (b, 0, v_off + p))],
        out_specs=pl.BlockSpec((1, S, PAIR_W), lambda b, p: (b, 0, p)),
        out_shape=jax.ShapeDtypeStruct((B, S, MOBA_W), BF16),
        scratch_shapes=[pltpu.VMEM((2, nb, HEAD_DIM + SUM_ROWS, MOBA_BLOCK), BF16)],
        name="moba",
    )(qkv, qkv, qkv)


def _ret_kernel(q_ref, k_ref, v_ref, g_ref, gn_ref, dec_ref, zeta_ref, xi_ref, cdec_ref, o_ref):
    S = q_ref.shape[1]
    C = RET_CHUNK
    nc = S // C
    lane = lax.broadcasted_iota(jnp.int32, (1, PAIR_W), 1)
    state = [None, None]
    for n in range(nc):
        rows = slice(n * C, (n + 1) * C)
        q = q_ref[0, rows, :]
        kc = k_ref[0, rows, :]
        kt = kc.T.astype(F32)
        for h in range(2):
            hm = (lane >= h * HEAD_DIM) & (lane < (h + 1) * HEAD_DIM)
            cols = slice(h * RET_V_DIM, (h + 1) * RET_V_DIM)
            qh = jnp.where(hm, q, jnp.zeros_like(q))
            vh = v_ref[0, rows, cols]
            s = lax.dot_general(qh, kc, NT_DIMS, preferred_element_type=F32)
            p = (s * dec_ref[h]).astype(BF16)
            o = jnp.dot(p, vh, preferred_element_type=F32)
            if n > 0:
                o = o + jnp.dot(qh, state[h].astype(BF16), preferred_element_type=F32) * xi_ref[h]
            if n < nc - 1:
                kz = (kt * zeta_ref[h]).astype(BF16)
                kv = jnp.dot(kz, vh, preferred_element_type=F32)
                state[h] = kv if n == 0 else state[h] * cdec_ref[h] + kv
            mu = jnp.mean(o, axis=-1, keepdims=True)
            d = o - mu
            var = jnp.mean(d * d, axis=-1, keepdims=True)
            y = d * lax.rsqrt(var + EPS) * gn_ref[:, cols]
            g = g_ref[0, rows, cols].astype(F32)
            o_ref[0, rows, cols] = (g * _sigmoid(g) * y).astype(BF16)


def _ret_tables():
    C = RET_CHUNK
    log_g = jnp.log(1.0 - jnp.power(2.0, -5.0 - jnp.arange(HEADS, dtype=F32)))
    i = jnp.arange(C, dtype=F32)
    diff = i[:, None] - i[None, :]
    dec = jnp.where(diff >= 0, jnp.exp(jnp.maximum(diff, 0.0)[None] * log_g[:, None, None]), 0.0)
    zeta = jnp.exp((C - 1 - i)[None, :] * log_g[:, None])[:, None, :]
    xi = jnp.exp((i + 1)[None, :] * log_g[:, None])
    xi = jnp.broadcast_to(xi[:, :, None], (HEADS, C, RET_V_DIM))
    cdec = jnp.broadcast_to(jnp.exp(C * log_g)[:, None, None], (HEADS, 1, RET_V_DIM))
    return dec, zeta, xi, cdec


def _ret(qkv, gates, gn_w):
    B, S, _ = qkv.shape
    C = RET_CHUNK
    q_off = 3 * MOBA_W // PAIR_W
    k_off = 4 * MOBA_W // PAIR_W
    v_off = 5 * MOBA_W // (2 * RET_V_DIM)
    dec, zeta, xi, cdec = _ret_tables()
    return pl.pallas_call(
        _ret_kernel,
        grid=(B, N_PAIRS),
        in_specs=[pl.BlockSpec((1, S, PAIR_W), lambda b, p: (b, 0, q_off + p)),
                  pl.BlockSpec((1, S, PAIR_W), lambda b, p: (b, 0, k_off + p)),
                  pl.BlockSpec((1, S, 2 * RET_V_DIM), lambda b, p: (b, 0, v_off + p)),
                  pl.BlockSpec((1, S, 2 * RET_V_DIM), lambda b, p: (b, 0, p)),
                  pl.BlockSpec((1, 2 * RET_V_DIM), lambda b, p: (0, p)),
                  pl.BlockSpec((2, C, C), lambda b, p: (p, 0, 0)),
                  pl.BlockSpec((2, 1, C), lambda b, p: (p, 0, 0)),
                  pl.BlockSpec((2, C, RET_V_DIM), lambda b, p: (p, 0, 0)),
                  pl.BlockSpec((2, 1, RET_V_DIM), lambda b, p: (p, 0, 0))],
        out_specs=pl.BlockSpec((1, S, 2 * RET_V_DIM), lambda b, p: (b, 0, p)),
        out_shape=jax.ShapeDtypeStruct((B, S, RET_V_W), BF16),
        name="ret",
    )(qkv, qkv, qkv, gates, gn_w.reshape(1, RET_V_W), dec, zeta, xi, cdec)


def _out_mlp_kernel(x_ref, a_ref, r_ref, ga_ref, gr_ref, ada_ref, ln2_ref, fw_ref,
                    wmo_ref, wro_ref, wout_ref, w1_ref, w2_ref, o_ref):
    x = x_ref[0]
    ya = jnp.dot(a_ref[0], wmo_ref[...], preferred_element_type=F32)
    yr = jnp.dot(r_ref[0], wro_ref[...], preferred_element_type=F32)
    merged = (_sigmoid(ga_ref[0].astype(F32)) * ya
              + _sigmoid(gr_ref[0].astype(F32)) * yr)
    mo = jnp.dot(merged.astype(BF16), wout_ref[...], preferred_element_type=F32)
    gate1 = ada_ref[0, 2:3, :]
    shift2 = ada_ref[0, 3:4, :]
    scale2 = ada_ref[0, 4:5, :]
    gate2 = ada_ref[0, 5:6, :]
    x1 = x + gate1 * mo
    h2 = (_rms_norm(x1, ln2_ref[...]) * (1.0 + scale2) + shift2).astype(BF16)
    acc = jnp.zeros_like(x1)
    for ci in range(D_FF // FF_CHUNK):
        c0 = ci * FF_CHUNK
        hid = jnp.dot(h2, w1_ref[:, c0:c0 + FF_CHUNK], preferred_element_type=F32)
        hid = jnp.square(jnp.maximum(hid, 0.0)).astype(BF16)
        acc = acc + jnp.dot(hid, w2_ref[c0:c0 + FF_CHUNK, :], preferred_element_type=F32)
    x2 = x1 + gate2 * acc
    o_ref[0] = _rms_norm(x2, fw_ref[...])


def _out_mlp(x, a_out, r_out, gates, ada3, ln2_w, final_w, wmo, wro, wout, w1, w2):
    B, S, D = x.shape
    tm = TOKEN_TILE
    const = functools.partial(pl.BlockSpec, pipeline_mode=pl.Buffered(1))
    return pl.pallas_call(
        _out_mlp_kernel,
        grid=(B, S // tm),
        in_specs=[pl.BlockSpec((1, tm, D), lambda b, si: (b, si, 0)),
                  pl.BlockSpec((1, tm, MOBA_W), lambda b, si: (b, si, 0)),
                  pl.BlockSpec((1, tm, RET_V_W), lambda b, si: (b, si, 0)),
                  pl.BlockSpec((1, tm, D), lambda b, si: (b, si, 1)),
                  pl.BlockSpec((1, tm, D), lambda b, si: (b, si, 2)),
                  pl.BlockSpec((1, N_ADA, D), lambda b, si: (b, 0, 0)),
                  pl.BlockSpec((1, D), lambda b, si: (0, 0)),
                  pl.BlockSpec((1, D), lambda b, si: (0, 0)),
                  const((MOBA_W, D), lambda b, si: (0, 0)),
                  const((RET_V_W, D), lambda b, si: (0, 0)),
                  const((D, D), lambda b, si: (0, 0)),
                  const((D, D_FF), lambda b, si: (0, 0)),
                  const((D_FF, D), lambda b, si: (0, 0))],
        out_specs=pl.BlockSpec((1, tm, D), lambda b, si: (b, si, 0)),
        out_shape=jax.ShapeDtypeStruct((B, S, D), F32),
        compiler_params=pltpu.CompilerParams(vmem_limit_bytes=VMEM_LIMIT),
        name="out_mlp",
    )(x, a_out, r_out, gates, gates, ada3, ln2_w.reshape(1, D), final_w.reshape(1, D),
      wmo, wro, wout, w1, w2)


def _rotary_tables(S, inv_freq, scale):
    pos = jnp.arange(S, dtype=F32)
    ang = pos[:, None] * inv_freq[None, :]
    cos, sin = jnp.cos(ang), jnp.sin(ang)
    zero = jnp.zeros_like(sin)
    reps = PAIR_W // HEAD_DIM
    cos_t = jnp.tile(jnp.concatenate([cos, cos], axis=1), (1, reps))
    sin_up = jnp.tile(jnp.concatenate([zero, sin], axis=1), (1, reps))
    sin_lo = jnp.tile(jnp.concatenate([-sin, zero], axis=1), (1, reps))
    return jnp.stack([cos_t, sin_up, sin_lo]) * scale


def kernel(x, c, ln1_w, ln2_w, w_ada, b_ada, w_in, ret_gn_w, w_moba_o, w_ret_o,
           w_out, w_ff1, w_ff2, final_norm_w):
    B, S, D = x.shape
    assert w_in.shape[0] == 1, "final norm is fused into the single layer's MLP kernel"
    moba_inv = 1.0 / (ROPE_THETA ** (jnp.arange(0, HEAD_DIM, 2, dtype=F32) / HEAD_DIM))
    ret_inv = 1.0 / (ROPE_THETA ** jnp.linspace(0.0, 1.0, HEAD_DIM // 2, dtype=F32))
    qk_scale = HEAD_DIM ** -0.5
    tabs = jnp.stack([_rotary_tables(S, moba_inv, qk_scale * LOG2E),
                      _rotary_tables(S, moba_inv, 1.0),
                      _rotary_tables(S, ret_inv, 1.0),
                      _rotary_tables(S, ret_inv, qk_scale)])

    ada3 = _ada(c, w_ada[0], b_ada[0]).reshape(B, N_ADA, D)
    qkv, gates = _in_proj(x, ada3, ln1_w[0], tabs, w_in[0].astype(BF16))
    a_out = _moba(qkv)
    r_out = _ret(qkv, gates, ret_gn_w[0])
    return _out_mlp(x, a_out, r_out, gates, ada3, ln2_w[0], final_norm_w,
                    w_moba_o[0].astype(BF16), w_ret_o[0].astype(BF16), w_out[0].astype(BF16),
                    w_ff1[0].astype(BF16), w_ff2[0].astype(BF16))
```

```python
import functools
import math

import jax
import jax.numpy as jnp
from jax import lax
from jax.experimental import pallas as pl
from jax.experimental.pallas import tpu as pltpu

F32 = jnp.float32
BF16 = jnp.bfloat16

D_MODEL = 1024
N_ADA = 6
HEAD_DIM = 64
HEADS = 8
PAIR_W = 2 * HEAD_DIM
N_PAIRS = HEADS // 2
MOBA_W = HEADS * HEAD_DIM
MOBA_BLOCK = 256
MOBA_TOPK = 3
SCORE_LOOKAHEAD = 2
SUM_ROWS = 16
RET_V_DIM = 128
RET_V_W = HEADS * RET_V_DIM
RET_CHUNK = 256
D_FF = 4 * D_MODEL
FF_CHUNK = 1024
ROPE_THETA = 10000.0
EPS = 1e-6
LOG2E = math.log2(math.e)
QKV_W = 3 * MOBA_W + 2 * MOBA_W + RET_V_W
GATES_W = RET_V_W + 2 * D_MODEL
IN_WIDTH = QKV_W + GATES_W
PROJ_CHUNK = 512
TOKEN_TILE = 512
ROW_GROUP_EIGHTHS = (4, 8)
VMEM_LIMIT = 60 * 1024 * 1024

NT_DIMS = (((1,), (1,)), ((), ()))


def _sigmoid(t):
    return 1.0 / (1.0 + jnp.exp(-t))


def _rms_norm(x, w):
    return x * lax.rsqrt(jnp.mean(x * x, axis=-1, keepdims=True) + EPS) * w


def _ada_kernel(c_ref, w_ref, b_ref, o_ref):
    c = c_ref[...]
    ca = c * _sigmoid(c)
    o_ref[...] = jnp.dot(ca, w_ref[...], precision=lax.Precision.HIGHEST,
                         preferred_element_type=F32) + b_ref[...]


def _ada(c, w_ada, b_ada):
    B, D = c.shape
    N = w_ada.shape[1]
    bn = 1024
    return pl.pallas_call(
        _ada_kernel,
        grid=(N // bn,),
        in_specs=[pl.BlockSpec((B, D), lambda j: (0, 0)),
                  pl.BlockSpec((D, bn), lambda j: (0, j)),
                  pl.BlockSpec((1, bn), lambda j: (0, j))],
        out_specs=pl.BlockSpec((B, bn), lambda j: (0, j)),
        out_shape=jax.ShapeDtypeStruct((B, N), F32),
        name="ada",
    )(c, w_ada, b_ada.reshape(1, N))


def _rotary(tk, tab_ref, which, rows):
    return (tk * tab_ref[which, 0, rows, :]
            + pltpu.roll(tk, HEAD_DIM // 2, 1) * tab_ref[which, 1, rows, :]
            + pltpu.roll(tk, PAIR_W - HEAD_DIM // 2, 1) * tab_ref[which, 2, rows, :])


def _staggered(gens):
    gens, live = list(gens), []
    while gens or live:
        if gens:
            live.append(gens.pop(0))
        for gen in list(live):
            if next(gen, StopIteration) is StopIteration:
                live.remove(gen)


def _row_groups(tm):
    bounds = [0] + [tm * f // 8 for f in ROW_GROUP_EIGHTHS]
    return [slice(lo, hi) for lo, hi in zip(bounds[:-1], bounds[1:])]


def _in_proj_kernel(x_ref, ada_ref, ln_ref, tab_ref, w_ref, qkv_ref, gates_ref):
    shift = ada_ref[0, 0:1, :]
    scale = ada_ref[0, 1:2, :]
    rot = {0: 0, 1: 1, 3: 2, 4: 3}

    def stages(rows):
        h = (_rms_norm(x_ref[0, rows, :], ln_ref[...]) * (1.0 + scale) + shift).astype(BF16)
        yield
        for ci in range(IN_WIDTH // PROJ_CHUNK):
            c0 = ci * PROJ_CHUNK
            t = jnp.dot(h, w_ref[:, c0:c0 + PROJ_CHUNK], preferred_element_type=F32)
            if ci in rot:
                for i in range(PROJ_CHUNK // PAIR_W):
                    lo = i * PAIR_W
                    piece = _rotary(t[:, lo:lo + PAIR_W], tab_ref, rot[ci], rows)
                    qkv_ref[0, rows, c0 + lo:c0 + lo + PAIR_W] = piece.astype(BF16)
            elif c0 < QKV_W:
                qkv_ref[0, rows, c0:c0 + PROJ_CHUNK] = t.astype(BF16)
            else:
                gates_ref[0, rows, c0 - QKV_W:c0 - QKV_W + PROJ_CHUNK] = t.astype(BF16)
            yield

    _staggered(stages(rows) for rows in _row_groups(x_ref.shape[1]))


def _in_proj(x, ada3, ln1_w, tabs, w_in_bf16):
    B, S, D = x.shape
    tm = TOKEN_TILE
    return pl.pallas_call(
        _in_proj_kernel,
        grid=(S // tm, B),
        in_specs=[pl.BlockSpec((1, tm, D), lambda si, b: (b, si, 0)),
                  pl.BlockSpec((1, N_ADA, D), lambda si, b: (b, 0, 0)),
                  pl.BlockSpec((1, D), lambda si, b: (0, 0)),
                  pl.BlockSpec((4, 3, tm, PAIR_W), lambda si, b: (0, 0, si, 0)),
                  pl.BlockSpec((D, IN_WIDTH), lambda si, b: (0, 0),
                               pipeline_mode=pl.Buffered(1))],
        out_specs=[pl.BlockSpec((1, tm, QKV_W), lambda si, b: (b, si, 0)),
                   pl.BlockSpec((1, tm, GATES_W), lambda si, b: (b, si, 0))],
        out_shape=[jax.ShapeDtypeStruct((B, S, QKV_W), BF16),
                   jax.ShapeDtypeStruct((B, S, GATES_W), BF16)],
        compiler_params=pltpu.CompilerParams(vmem_limit_bytes=VMEM_LIMIT),
        name="in_proj",
    )(x, ada3, ln1_w.reshape(1, D), tabs, w_in_bf16)


def _moba_kernel(q_ref, k_ref, v_ref, o_ref, vt_scr):
    S = q_ref.shape[1]
    T = MOBA_BLOCK
    nb = S // T
    lane = lax.broadcasted_iota(jnp.int32, (1, PAIR_W), 1)
    head_masks = [(lane >= h * HEAD_DIM) & (lane < (h + 1) * HEAD_DIM) for h in range(2)]
    krow = lax.broadcasted_iota(jnp.int32, (T, 2 * T), 0)
    qcol = lax.broadcasted_iota(jnp.int32, (T, 2 * T), 1) % T
    causal2 = krow <= qcol

    def score_steps(j, st):
        qt = q_ref[0, j * T:(j + 1) * T, :]
        zero = jnp.zeros_like(qt)
        q2 = jnp.concatenate([jnp.where(hm, qt, zero) for hm in head_masks], axis=0)
        st["blocks"], st["cmax"] = [], []
        for n in range(j + 1):
            def step(n=n):
                sn = lax.dot_general(k_ref[0, n * T:(n + 1) * T, :], q2, NT_DIMS,
                                     preferred_element_type=F32)
                if n == j:
                    sn = jnp.where(causal2, sn, -jnp.inf)
                st["blocks"].append(sn)
                st["cmax"].append(jnp.max(sn, axis=0, keepdims=True))
            yield step

    biases = []

    def prologue_steps():
        for n in range(nb):
            vt = v_ref[0, n * T:(n + 1) * T, :].T
            for h in range(2):
                vt_scr[h, n, 0:HEAD_DIM, :] = vt[h * HEAD_DIM:(h + 1) * HEAD_DIM]
                vt_scr[h, n, HEAD_DIM:, :] = jnp.ones((SUM_ROWS, T), BF16)
            if n % 2:
                yield
        kmean = jnp.mean(k_ref[0].astype(F32).reshape(nb, T, PAIR_W), axis=1)
        km_rows = []
        for hm in head_masks:
            kmh = jnp.where(hm, kmean, 0.0)
            hi = kmh.astype(BF16).astype(F32)
            mid = (kmh - hi).astype(BF16).astype(F32)
            km_rows += [hi, mid, kmh - hi - mid]
        km = jnp.concatenate(km_rows, axis=0).astype(BF16)
        gate_terms = lax.dot_general(km, q_ref[0], NT_DIMS,
                                     preferred_element_type=F32)
        yield
        blk = lax.broadcasted_iota(jnp.int32, (nb, S), 0)
        qblk = lax.broadcasted_iota(jnp.int32, (nb, S), 1) // T
        past = blk < qblk
        for h in range(2):
            gt = gate_terms[3 * nb * h:3 * nb * (h + 1)]
            gate = gt[0:nb] + gt[nb:2 * nb] + gt[2 * nb:3 * nb]
            g = jnp.where(past, gate, -jnp.inf)
            rank = jnp.zeros((nb, S), jnp.int32)
            for m in range(nb - 1):
                gm = g[m:m + 1, :]
                beats = (gm > g) | ((gm == g) & (m < blk))
                rank = rank + beats.astype(jnp.int32)
            sel = past & (rank < MOBA_TOPK)
            biases.append(jnp.where(sel, 0.0, -jnp.inf).astype(F32))
            yield

    order = list(range(nb - 1, -1, -1))
    states = [dict() for _ in range(nb)]
    first = [step for j in order[:SCORE_LOOKAHEAD] for step in score_steps(j, states[j])]
    pro = prologue_steps()
    for step in first:
        step()
        next(pro, None)
    for _ in pro:
        pass

    def value_steps(j, st):
        bias = jnp.concatenate([b[:, j * T:(j + 1) * T] for b in biases], axis=1)
        m = st["cmax"][j]
        for n in range(j):
            m = jnp.maximum(m, st["cmax"][n] + bias[n:n + 1])
        acc = [jnp.zeros((HEAD_DIM + SUM_ROWS, T), F32) for _ in range(2)]
        for n in range(j + 1):
            def step(n=n):
                off = m if n == j else m - bias[n:n + 1]
                p = jnp.exp2((st["blocks"][n] - off).astype(BF16))
                for h in range(2):
                    acc[h] = acc[h] + jnp.dot(vt_scr[h, n], p[:, h * T:(h + 1) * T],
                                              preferred_element_type=F32)
            yield step

        def finish():
            outs = [a[0:HEAD_DIM] * (1.0 / a[HEAD_DIM:HEAD_DIM + 1]) for a in acc]
            o_ref[0, j * T:(j + 1) * T, :] = jnp.concatenate(outs, axis=0).T.astype(BF16)
        yield finish

    for i, j in enumerate(order):
        ahead = i + SCORE_LOOKAHEAD
        nxt = list(score_steps(order[ahead], states[order[ahead]])) if ahead < nb else []
        cur = list(value_steps(j, states[j]))
        for t in range(max(len(nxt), len(cur))):
            if t < len(nxt):
                nxt[t]()
            if t < len(cur):
                cur[t]()
        states[j].clear()


def _moba(qkv):
    B, S, _ = qkv.shape
    nb = S // MOBA_BLOCK
    k_off = MOBA_W // PAIR_W
    v_off = 2 * MOBA_W // PAIR_W
    return pl.pallas_call(
        _moba_kernel,
        grid=(B, N_PAIRS),
        in_specs=[pl.BlockSpec((1, S, PAIR_W), lambda b, p: (b, 0, p)),
                  pl.BlockSpec((1, S, PAIR_W), lambda b, p: (b, 0, k_off + p)),
                  pl.BlockSpec((1, S, PAIR_W), lambda b, p: (b, 0, v_off + p))],
        out_specs=pl.BlockSpec((1, S, PAIR_W), lambda b, p: (b, 0, p)),
        out_shape=jax.ShapeDtypeStruct((B, S, MOBA_W), BF16),
        scratch_shapes=[pltpu.VMEM((2, nb, HEAD_DIM + SUM_ROWS, MOBA_BLOCK), BF16)],
        name="moba",
    )(qkv, qkv, qkv)


def _ret_kernel(q_ref, k_ref, v_ref, g_ref, gn_ref, dec_ref, zeta_ref, xi_ref, cdec_ref, o_ref):
    S = q_ref.shape[1]
    C = RET_CHUNK
    nc = S // C
    lane = lax.broadcasted_iota(jnp.int32, (1, PAIR_W), 1)
    state = [None, None]
    for n in range(nc):
        rows = slice(n * C, (n + 1) * C)
        q = q_ref[0, rows, :]
        kc = k_ref[0, rows, :]
        kt = kc.T.astype(F32)
        for h in range(2):
            hm = (lane >= h * HEAD_DIM) & (lane < (h + 1) * HEAD_DIM)
            cols = slice(h * RET_V_DIM, (h + 1) * RET_V_DIM)
            qh = jnp.where(hm, q, jnp.zeros_like(q))
            vh = v_ref[0, rows, cols]
            s = lax.dot_general(qh, kc, NT_DIMS, preferred_element_type=F32)
            p = (s * dec_ref[h]).astype(BF16)
            o = jnp.dot(p, vh, preferred_element_type=F32)
            if n > 0:
                o = o + jnp.dot(qh, state[h].astype(BF16), preferred_element_type=F32) * xi_ref[h]
            if n < nc - 1:
                kz = (kt * zeta_ref[h]).astype(BF16)
                kv = jnp.dot(kz, vh, preferred_element_type=F32)
                state[h] = kv if n == 0 else state[h] * cdec_ref[h] + kv
            mu = jnp.mean(o, axis=-1, keepdims=True)
            d = o - mu
            var = jnp.mean(d * d, axis=-1, keepdims=True)
            y = d * lax.rsqrt(var + EPS) * gn_ref[:, cols]
            g = g_ref[0, rows, cols].astype(F32)
            o_ref[0, rows, cols] = (g * _sigmoid(g) * y).astype(BF16)


def _ret_tables():
    C = RET_CHUNK
    log_g = jnp.log(1.0 - jnp.power(2.0, -5.0 - jnp.arange(HEADS, dtype=F32)))
    i = jnp.arange(C, dtype=F32)
    diff = i[:, None] - i[None, :]
    dec = jnp.where(diff >= 0, jnp.exp(jnp.maximum(diff, 0.0)[None] * log_g[:, None, None]), 0.0)
    zeta = jnp.exp((C - 1 - i)[None, :] * log_g[:, None])[:, None, :]
    xi = jnp.exp((i + 1)[None, :] * log_g[:, None])
    xi = jnp.broadcast_to(xi[:, :, None], (HEADS, C, RET_V_DIM))
    cdec = jnp.broadcast_to(jnp.exp(C * log_g)[:, None, None], (HEADS, 1, RET_V_DIM))
    return dec, zeta, xi, cdec


def _ret(qkv, gates, gn_w):
    B, S, _ = qkv.shape
    C = RET_CHUNK
    q_off = 3 * MOBA_W // PAIR_W
    k_off = 4 * MOBA_W // PAIR_W
    v_off = 5 * MOBA_W // (2 * RET_V_DIM)
    dec, zeta, xi, cdec = _ret_tables()
    return pl.pallas_call(
        _ret_kernel,
        grid=(B, N_PAIRS),
        in_specs=[pl.BlockSpec((1, S, PAIR_W), lambda b, p: (b, 0, q_off + p)),
                  pl.BlockSpec((1, S, PAIR_W), lambda b, p: (b, 0, k_off + p)),
                  pl.BlockSpec((1, S, 2 * RET_V_DIM), lambda b, p: (b, 0, v_off + p)),
                  pl.BlockSpec((1, S, 2 * RET_V_DIM), lambda b, p: (b, 0, p)),
                  pl.BlockSpec((1, 2 * RET_V_DIM), lambda b, p: (0, p)),
                  pl.BlockSpec((2, C, C), lambda b, p: (p, 0, 0)),
                  pl.BlockSpec((2, 1, C), lambda b, p: (p, 0, 0)),
                  pl.BlockSpec((2, C, RET_V_DIM), lambda b, p: (p, 0, 0)),
                  pl.BlockSpec((2, 1, RET_V_DIM), lambda b, p: (p, 0, 0))],
        out_specs=pl.BlockSpec((1, S, 2 * RET_V_DIM), lambda b, p: (b, 0, p)),
        out_shape=jax.ShapeDtypeStruct((B, S, RET_V_W), BF16),
        name="ret",
    )(qkv, qkv, qkv, gates, gn_w.reshape(1, RET_V_W), dec, zeta, xi, cdec)


def _out_mlp_kernel(x_ref, a_ref, r_ref, ga_ref, gr_ref, ada_ref, ln2_ref, fw_ref,
                    wmo_ref, wro_ref, wout_ref, w1_ref, w2_ref, o_ref):
    gate1 = ada_ref[0, 2:3, :]
    shift2 = ada_ref[0, 3:4, :]
    scale2 = ada_ref[0, 4:5, :]
    gate2 = ada_ref[0, 5:6, :]

    def stages(rows):
        ya = jnp.dot(a_ref[0, rows, :], wmo_ref[...], preferred_element_type=F32)
        yr = jnp.dot(r_ref[0, rows, :], wro_ref[...], preferred_element_type=F32)
        yield
        merged = (_sigmoid(ga_ref[0, rows, :].astype(F32)) * ya
                  + _sigmoid(gr_ref[0, rows, :].astype(F32)) * yr).astype(BF16)
        yield
        mo = jnp.dot(merged, wout_ref[...], preferred_element_type=F32)
        yield
        x1 = x_ref[0, rows, :] + gate1 * mo
        h2 = (_rms_norm(x1, ln2_ref[...]) * (1.0 + scale2) + shift2).astype(BF16)
        yield
        acc = jnp.zeros_like(x1)
        for ci in range(D_FF // FF_CHUNK):
            c0 = ci * FF_CHUNK
            hid = jnp.dot(h2, w1_ref[:, c0:c0 + FF_CHUNK], preferred_element_type=F32)
            yield
            hid = jnp.square(jnp.maximum(hid, 0.0)).astype(BF16)
            yield
            acc = acc + jnp.dot(hid, w2_ref[c0:c0 + FF_CHUNK, :], preferred_element_type=F32)
            yield
        x2 = x1 + gate2 * acc
        o_ref[0, rows, :] = _rms_norm(x2, fw_ref[...])

    _staggered(stages(rows) for rows in _row_groups(x_ref.shape[1]))


def _out_mlp(x, a_out, r_out, gates, ada3, ln2_w, final_w, wmo, wro, wout, w1, w2):
    B, S, D = x.shape
    tm = TOKEN_TILE
    const = functools.partial(pl.BlockSpec, pipeline_mode=pl.Buffered(1))
    return pl.pallas_call(
        _out_mlp_kernel,
        grid=(B, S // tm),
        in_specs=[pl.BlockSpec((1, tm, D), lambda b, si: (b, si, 0)),
                  pl.BlockSpec((1, tm, MOBA_W), lambda b, si: (b, si, 0)),
                  pl.BlockSpec((1, tm, RET_V_W), lambda b, si: (b, si, 0)),
                  pl.BlockSpec((1, tm, D), lambda b, si: (b, si, 1)),
                  pl.BlockSpec((1, tm, D), lambda b, si: (b, si, 2)),
                  pl.BlockSpec((1, N_ADA, D), lambda b, si: (b, 0, 0)),
                  pl.BlockSpec((1, D), lambda b, si: (0, 0)),
                  pl.BlockSpec((1, D), lambda b, si: (0, 0)),
                  const((MOBA_W, D), lambda b, si: (0, 0)),
                  const((RET_V_W, D), lambda b, si: (0, 0)),
                  const((D, D), lambda b, si: (0, 0)),
                  const((D, D_FF), lambda b, si: (0, 0)),
                  const((D_FF, D), lambda b, si: (0, 0))],
        out_specs=pl.BlockSpec((1, tm, D), lambda b, si: (b, si, 0)),
        out_shape=jax.ShapeDtypeStruct((B, S, D), F32),
        compiler_params=pltpu.CompilerParams(vmem_limit_bytes=VMEM_LIMIT),
        name="out_mlp",
    )(x, a_out, r_out, gates, gates, ada3, ln2_w.reshape(1, D), final_w.reshape(1, D),
      wmo, wro, wout, w1, w2)


def _rotary_tables(S, inv_freq, scale):
    pos = jnp.arange(S, dtype=F32)
    ang = pos[:, None] * inv_freq[None, :]
    cos, sin = jnp.cos(ang), jnp.sin(ang)
    zero = jnp.zeros_like(sin)
    reps = PAIR_W // HEAD_DIM
    cos_t = jnp.tile(jnp.concatenate([cos, cos], axis=1), (1, reps))
    sin_up = jnp.tile(jnp.concatenate([zero, sin], axis=1), (1, reps))
    sin_lo = jnp.tile(jnp.concatenate([-sin, zero], axis=1), (1, reps))
    return jnp.stack([cos_t, sin_up, sin_lo]) * scale


def kernel(x, c, ln1_w, ln2_w, w_ada, b_ada, w_in, ret_gn_w, w_moba_o, w_ret_o,
           w_out, w_ff1, w_ff2, final_norm_w):
    B, S, D = x.shape
    assert w_in.shape[0] == 1, "final norm is fused into the single layer's MLP kernel"
    moba_inv = 1.0 / (ROPE_THETA ** (jnp.arange(0, HEAD_DIM, 2, dtype=F32) / HEAD_DIM))
    ret_inv = 1.0 / (ROPE_THETA ** jnp.linspace(0.0, 1.0, HEAD_DIM // 2, dtype=F32))
    qk_scale = HEAD_DIM ** -0.5
    tabs = jnp.stack([_rotary_tables(S, moba_inv, qk_scale * LOG2E),
                      _rotary_tables(S, moba_inv, 1.0),
                      _rotary_tables(S, ret_inv, 1.0),
                      _rotary_tables(S, ret_inv, qk_scale)])

    ada3 = _ada(c, w_ada[0], b_ada[0]).reshape(B, N_ADA, D)
    qkv, gates = _in_proj(x, ada3, ln1_w[0], tabs, w_in[0].astype(BF16))
    a_out = _moba(qkv)
    r_out = _ret(qkv, gates, ret_gn_w[0])
    return _out_mlp(x, a_out, r_out, gates, ada3, ln2_w[0], final_norm_w,
                    w_moba_o[0].astype(BF16), w_ret_o[0].astype(BF16), w_out[0].astype(BF16),
                    w_ff1[0].astype(BF16), w_ff2[0].astype(BF16))
```

```python
import functools
import math

import jax
import jax.numpy as jnp
from jax import lax
from jax.experimental import pallas as pl
from jax.experimental.pallas import tpu as pltpu

F32 = jnp.float32
BF16 = jnp.bfloat16

D_MODEL = 1024
N_ADA = 6
HEAD_DIM = 64
HEADS = 8
PAIR_W = 2 * HEAD_DIM
N_PAIRS = HEADS // 2
MOBA_W = HEADS * HEAD_DIM
MOBA_BLOCK = 256
MOBA_TOPK = 3
SCORE_LOOKAHEAD = 2
SUM_ROWS = 16
RET_V_DIM = 128
RET_V_W = HEADS * RET_V_DIM
RET_CHUNK = 256
D_FF = 4 * D_MODEL
FF_CHUNK = 1024
ROPE_THETA = 10000.0
EPS = 1e-6
LOG2E = math.log2(math.e)
QKV_W = 3 * MOBA_W + 2 * MOBA_W + RET_V_W
GATES_W = RET_V_W + 2 * D_MODEL
IN_WIDTH = QKV_W + GATES_W
PROJ_CHUNK = 512
TOKEN_TILE = 512
ROW_GROUP_EIGHTHS = (4, 8)
VMEM_LIMIT = 60 * 1024 * 1024

NT_DIMS = (((1,), (1,)), ((), ()))


def _sigmoid(t):
    return 1.0 / (1.0 + jnp.exp(-t))


def _rms_norm(x, w):
    return x * lax.rsqrt(jnp.mean(x * x, axis=-1, keepdims=True) + EPS) * w


def _ada_kernel(c_ref, w_ref, b_ref, o_ref):
    c = c_ref[...]
    ca = c * _sigmoid(c)
    o_ref[...] = jnp.dot(ca, w_ref[...], precision=lax.Precision.HIGHEST,
                         preferred_element_type=F32) + b_ref[...]


def _ada(c, w_ada, b_ada):
    B, D = c.shape
    N = w_ada.shape[1]
    bn = 1024
    return pl.pallas_call(
        _ada_kernel,
        grid=(N // bn,),
        in_specs=[pl.BlockSpec((B, D), lambda j: (0, 0)),
                  pl.BlockSpec((D, bn), lambda j: (0, j)),
                  pl.BlockSpec((1, bn), lambda j: (0, j))],
        out_specs=pl.BlockSpec((B, bn), lambda j: (0, j)),
        out_shape=jax.ShapeDtypeStruct((B, N), F32),
        name="ada",
    )(c, w_ada, b_ada.reshape(1, N))


def _rotary(tk, tab_ref, which, rows):
    return (tk * tab_ref[which, 0, rows, :]
            + pltpu.roll(tk, HEAD_DIM // 2, 1) * tab_ref[which, 1, rows, :]
            + pltpu.roll(tk, PAIR_W - HEAD_DIM // 2, 1) * tab_ref[which, 2, rows, :])


def _staggered(gens):
    gens, live = list(gens), []
    while gens or live:
        if gens:
            live.append(gens.pop(0))
        for gen in list(live):
            if next(gen, StopIteration) is StopIteration:
                live.remove(gen)


def _row_groups(tm):
    bounds = [0] + [tm * f // 8 for f in ROW_GROUP_EIGHTHS]
    return [slice(lo, hi) for lo, hi in zip(bounds[:-1], bounds[1:])]


def _in_proj_kernel(x_ref, ada_ref, ln_ref, tab_ref, w_ref, qkv_ref, gates_ref):
    shift = ada_ref[0, 0:1, :]
    scale = ada_ref[0, 1:2, :]
    rot = {0: 0, 1: 1, 3: 2, 4: 3}

    def stages(rows):
        h = (_rms_norm(x_ref[0, rows, :], ln_ref[...]) * (1.0 + scale) + shift).astype(BF16)
        yield
        for ci in range(IN_WIDTH // PROJ_CHUNK):
            c0 = ci * PROJ_CHUNK
            t = jnp.dot(h, w_ref[:, c0:c0 + PROJ_CHUNK], preferred_element_type=F32)
            if ci in rot:
                for i in range(PROJ_CHUNK // PAIR_W):
                    lo = i * PAIR_W
                    piece = _rotary(t[:, lo:lo + PAIR_W], tab_ref, rot[ci], rows)
                    qkv_ref[0, rows, c0 + lo:c0 + lo + PAIR_W] = piece.astype(BF16)
            elif c0 < QKV_W:
                qkv_ref[0, rows, c0:c0 + PROJ_CHUNK] = t.astype(BF16)
            else:
                gates_ref[0, rows, c0 - QKV_W:c0 - QKV_W + PROJ_CHUNK] = t.astype(BF16)
            yield

    _staggered(stages(rows) for rows in _row_groups(x_ref.shape[1]))


def _in_proj(x, ada3, ln1_w, tabs, w_in_bf16):
    B, S, D = x.shape
    tm = TOKEN_TILE
    return pl.pallas_call(
        _in_proj_kernel,
        grid=(S // tm, B),
        in_specs=[pl.BlockSpec((1, tm, D), lambda si, b: (b, si, 0)),
                  pl.BlockSpec((1, N_ADA, D), lambda si, b: (b, 0, 0)),
                  pl.BlockSpec((1, D), lambda si, b: (0, 0)),
                  pl.BlockSpec((4, 3, tm, PAIR_W), lambda si, b: (0, 0, si, 0)),
                  pl.BlockSpec((D, IN_WIDTH), lambda si, b: (0, 0),
                               pipeline_mode=pl.Buffered(1))],
        out_specs=[pl.BlockSpec((1, tm, QKV_W), lambda si, b: (b, si, 0)),
                   pl.BlockSpec((1, tm, GATES_W), lambda si, b: (b, si, 0))],
        out_shape=[jax.ShapeDtypeStruct((B, S, QKV_W), BF16),
                   jax.ShapeDtypeStruct((B, S, GATES_W), BF16)],
        compiler_params=pltpu.CompilerParams(vmem_limit_bytes=VMEM_LIMIT),
        name="in_proj",
    )(x, ada3, ln1_w.reshape(1, D), tabs, w_in_bf16)


def _moba_steps(q_ref, k_ref, v_ref, o_ref, vt_scr):
    S = q_ref.shape[1]
    T = MOBA_BLOCK
    nb = S // T
    lane = lax.broadcasted_iota(jnp.int32, (1, PAIR_W), 1)
    head_masks = [(lane >= h * HEAD_DIM) & (lane < (h + 1) * HEAD_DIM) for h in range(2)]
    krow = lax.broadcasted_iota(jnp.int32, (T, 2 * T), 0)
    qcol = lax.broadcasted_iota(jnp.int32, (T, 2 * T), 1) % T
    causal2 = krow <= qcol

    def score_steps(j, st):
        qt = q_ref[0, j * T:(j + 1) * T, :]
        zero = jnp.zeros_like(qt)
        q2 = jnp.concatenate([jnp.where(hm, qt, zero) for hm in head_masks], axis=0)
        st["blocks"], st["cmax"] = [], []
        for n in range(j + 1):
            def step(n=n):
                sn = lax.dot_general(k_ref[0, n * T:(n + 1) * T, :], q2, NT_DIMS,
                                     preferred_element_type=F32)
                if n == j:
                    sn = jnp.where(causal2, sn, -jnp.inf)
                st["blocks"].append(sn)
                st["cmax"].append(jnp.max(sn, axis=0, keepdims=True))
            yield step

    biases = []

    def prologue_steps():
        for n in range(nb):
            vt = v_ref[0, n * T:(n + 1) * T, :].T
            for h in range(2):
                vt_scr[h, n, 0:HEAD_DIM, :] = vt[h * HEAD_DIM:(h + 1) * HEAD_DIM]
                vt_scr[h, n, HEAD_DIM:, :] = jnp.ones((SUM_ROWS, T), BF16)
            if n % 2:
                yield
        kmean = jnp.mean(k_ref[0].astype(F32).reshape(nb, T, PAIR_W), axis=1)
        km_rows = []
        for hm in head_masks:
            kmh = jnp.where(hm, kmean, 0.0)
            hi = kmh.astype(BF16).astype(F32)
            mid = (kmh - hi).astype(BF16).astype(F32)
            km_rows += [hi, mid, kmh - hi - mid]
        km = jnp.concatenate(km_rows, axis=0).astype(BF16)
        gate_terms = lax.dot_general(km, q_ref[0], NT_DIMS,
                                     preferred_element_type=F32)
        yield
        blk = lax.broadcasted_iota(jnp.int32, (nb, S), 0)
        qblk = lax.broadcasted_iota(jnp.int32, (nb, S), 1) // T
        past = blk < qblk
        for h in range(2):
            gt = gate_terms[3 * nb * h:3 * nb * (h + 1)]
            gate = gt[0:nb] + gt[nb:2 * nb] + gt[2 * nb:3 * nb]
            g = jnp.where(past, gate, -jnp.inf)
            rank = jnp.zeros((nb, S), jnp.int32)
            for m in range(nb - 1):
                gm = g[m:m + 1, :]
                beats = (gm > g) | ((gm == g) & (m < blk))
                rank = rank + beats.astype(jnp.int32)
            sel = past & (rank < MOBA_TOPK)
            biases.append(jnp.where(sel, 0.0, -jnp.inf).astype(F32))
            yield

    order = list(range(nb - 1, -1, -1))
    states = [dict() for _ in range(nb)]
    first = [step for j in order[:SCORE_LOOKAHEAD] for step in score_steps(j, states[j])]
    pro = prologue_steps()
    for step in first:
        step()
        next(pro, None)
        yield
    for _ in pro:
        yield

    def value_steps(j, st):
        bias = jnp.concatenate([b[:, j * T:(j + 1) * T] for b in biases], axis=1)
        m = st["cmax"][j]
        for n in range(j):
            m = jnp.maximum(m, st["cmax"][n] + bias[n:n + 1])
        acc = [jnp.zeros((HEAD_DIM + SUM_ROWS, T), F32) for _ in range(2)]
        for n in range(j + 1):
            def step(n=n):
                off = m if n == j else m - bias[n:n + 1]
                p = jnp.exp2((st["blocks"][n] - off).astype(BF16))
                for h in range(2):
                    acc[h] = acc[h] + jnp.dot(vt_scr[h, n], p[:, h * T:(h + 1) * T],
                                              preferred_element_type=F32)
            yield step

        def finish():
            outs = [a[0:HEAD_DIM] * (1.0 / a[HEAD_DIM:HEAD_DIM + 1]) for a in acc]
            o_ref[0, j * T:(j + 1) * T, :] = jnp.concatenate(outs, axis=0).T.astype(BF16)
        yield finish

    for i, j in enumerate(order):
        ahead = i + SCORE_LOOKAHEAD
        nxt = list(score_steps(order[ahead], states[order[ahead]])) if ahead < nb else []
        cur = list(value_steps(j, states[j]))
        for t in range(max(len(nxt), len(cur))):
            if t < len(nxt):
                nxt[t]()
            if t < len(cur):
                cur[t]()
            yield
        states[j].clear()


def _ret_steps(q_ref, k_ref, v_ref, g_ref, gn_ref, dec_ref, zeta_ref, xi_ref, cdec_ref, o_ref):
    S = q_ref.shape[1]
    C = RET_CHUNK
    nc = S // C
    lane = lax.broadcasted_iota(jnp.int32, (1, PAIR_W), 1)
    state = [None, None]
    gains = [gn_ref[:, h * RET_V_DIM:(h + 1) * RET_V_DIM] * math.sqrt(RET_V_DIM) for h in range(2)]
    for n in range(nc):
        rows = slice(n * C, (n + 1) * C)
        q = q_ref[0, rows, :]
        kc = k_ref[0, rows, :]
        kt = kc.T
        for h in range(2):
            hm = (lane >= h * HEAD_DIM) & (lane < (h + 1) * HEAD_DIM)
            cols = slice(h * RET_V_DIM, (h + 1) * RET_V_DIM)
            qh = jnp.where(hm, q, jnp.zeros_like(q))
            vh = v_ref[0, rows, cols]
            s = lax.dot_general(qh, kc, NT_DIMS, preferred_element_type=F32)
            p = s.astype(BF16) * dec_ref[h]
            o = jnp.dot(p, vh, preferred_element_type=F32)
            if n > 0:
                o = o + jnp.dot(qh, state[h].astype(BF16), preferred_element_type=F32) * xi_ref[h]
            if n < nc - 1:
                kz = kt * zeta_ref[h]
                kv = jnp.dot(kz, vh, preferred_element_type=F32)
                state[h] = kv if n == 0 else state[h] * cdec_ref[h] + kv
            yield
            d = o - jnp.sum(o, axis=-1, keepdims=True) * (1.0 / RET_V_DIM)
            r = lax.rsqrt(jnp.sum(d * d, axis=-1, keepdims=True) + RET_V_DIM * EPS)
            y = d * r * gains[h]
            hg = 0.5 * g_ref[0, rows, cols].astype(F32)
            o_ref[0, rows, cols] = (hg * (1.0 + jnp.tanh(hg)) * y).astype(BF16)
            yield


def _moba_kernel(*refs):
    for _ in _moba_steps(*refs):
        pass


def _ret_kernel(*refs):
    for _ in _ret_steps(*refs):
        pass


def _ret_tables():
    C = RET_CHUNK
    log_g = jnp.log(1.0 - jnp.power(2.0, -5.0 - jnp.arange(HEADS, dtype=F32)))
    i = jnp.arange(C, dtype=F32)
    diff = i[:, None] - i[None, :]
    dec = jnp.where(diff >= 0, jnp.exp(jnp.maximum(diff, 0.0)[None] * log_g[:, None, None]), 0.0)
    zeta = jnp.exp((C - 1 - i)[None, :] * log_g[:, None])[:, None, :]
    xi = jnp.exp((i + 1)[None, :] * log_g[:, None])
    xi = jnp.broadcast_to(xi[:, :, None], (HEADS, C, RET_V_DIM))
    cdec = jnp.broadcast_to(jnp.exp(C * log_g)[:, None, None], (HEADS, 1, RET_V_DIM))
    return dec.astype(BF16), zeta.astype(BF16), xi, cdec


def _pair_block(col_off):
    return lambda b, p: (b, 0, col_off // PAIR_W + p)


def _moba(qkv):
    B, S, _ = qkv.shape
    nb = S // MOBA_BLOCK
    return pl.pallas_call(
        _moba_kernel,
        grid=(B, N_PAIRS),
        in_specs=[pl.BlockSpec((1, S, PAIR_W), _pair_block(0)),
                  pl.BlockSpec((1, S, PAIR_W), _pair_block(MOBA_W)),
                  pl.BlockSpec((1, S, PAIR_W), _pair_block(2 * MOBA_W))],
        out_specs=pl.BlockSpec((1, S, PAIR_W), _pair_block(0)),
        out_shape=jax.ShapeDtypeStruct((B, S, MOBA_W), BF16),
        scratch_shapes=[pltpu.VMEM((2, nb, HEAD_DIM + SUM_ROWS, MOBA_BLOCK), BF16)],
        name="moba",
    )(qkv, qkv, qkv)


def _ret(qkv, gates, gn_w):
    B, S, _ = qkv.shape
    C = RET_CHUNK
    v_blk = 5 * MOBA_W // (2 * RET_V_DIM)
    dec, zeta, xi, cdec = _ret_tables()
    return pl.pallas_call(
        _ret_kernel,
        grid=(B, N_PAIRS),
        in_specs=[pl.BlockSpec((1, S, PAIR_W), _pair_block(3 * MOBA_W)),
                  pl.BlockSpec((1, S, PAIR_W), _pair_block(4 * MOBA_W)),
                  pl.BlockSpec((1, S, 2 * RET_V_DIM), lambda b, p: (b, 0, v_blk + p)),
                  pl.BlockSpec((1, S, 2 * RET_V_DIM), lambda b, p: (b, 0, p)),
                  pl.BlockSpec((1, 2 * RET_V_DIM), lambda b, p: (0, p)),
                  pl.BlockSpec((2, C, C), lambda b, p: (p, 0, 0)),
                  pl.BlockSpec((2, 1, C), lambda b, p: (p, 0, 0)),
                  pl.BlockSpec((2, C, RET_V_DIM), lambda b, p: (p, 0, 0)),
                  pl.BlockSpec((2, 1, RET_V_DIM), lambda b, p: (p, 0, 0))],
        out_specs=pl.BlockSpec((1, S, 2 * RET_V_DIM), lambda b, p: (b, 0, p)),
        out_shape=jax.ShapeDtypeStruct((B, S, RET_V_W), BF16),
        name="ret",
    )(qkv, qkv, qkv, gates, gn_w.reshape(1, RET_V_W), dec, zeta, xi, cdec)


def _out_mlp_kernel(x_ref, a_ref, r_ref, ga_ref, gr_ref, ada_ref, ln2_ref, fw_ref,
                    wmo_ref, wro_ref, wout_ref, w1_ref, w2_ref, o_ref):
    gate1 = ada_ref[0, 2:3, :]
    shift2 = ada_ref[0, 3:4, :]
    scale2 = ada_ref[0, 4:5, :]
    gate2 = ada_ref[0, 5:6, :]

    def stages(rows):
        ya = jnp.dot(a_ref[0, rows, :], wmo_ref[...], preferred_element_type=F32)
        yr = jnp.dot(r_ref[0, rows, :], wro_ref[...], preferred_element_type=F32)
        yield
        merged = (_sigmoid(ga_ref[0, rows, :].astype(F32)) * ya
                  + _sigmoid(gr_ref[0, rows, :].astype(F32)) * yr).astype(BF16)
        yield
        mo = jnp.dot(merged, wout_ref[...], preferred_element_type=F32)
        yield
        x1 = x_ref[0, rows, :] + gate1 * mo
        h2 = (_rms_norm(x1, ln2_ref[...]) * (1.0 + scale2) + shift2).astype(BF16)
        yield
        acc = jnp.zeros_like(x1)
        for ci in range(D_FF // FF_CHUNK):
            c0 = ci * FF_CHUNK
            hid = jnp.dot(h2, w1_ref[:, c0:c0 + FF_CHUNK], preferred_element_type=F32)
            yield
            hid = jnp.square(jnp.maximum(hid, 0.0)).astype(BF16)
            yield
            acc = acc + jnp.dot(hid, w2_ref[c0:c0 + FF_CHUNK, :], preferred_element_type=F32)
            yield
        x2 = x1 + gate2 * acc
        o_ref[0, rows, :] = _rms_norm(x2, fw_ref[...])

    _staggered(stages(rows) for rows in _row_groups(x_ref.shape[1]))


def _out_mlp(x, a_out, r_out, gates, ada3, ln2_w, final_w, wmo, wro, wout, w1, w2):
    B, S, D = x.shape
    tm = TOKEN_TILE
    const = functools.partial(pl.BlockSpec, pipeline_mode=pl.Buffered(1))
    return pl.pallas_call(
        _out_mlp_kernel,
        grid=(B, S // tm),
        in_specs=[pl.BlockSpec((1, tm, D), lambda b, si: (b, si, 0)),
                  pl.BlockSpec((1, tm, MOBA_W), lambda b, si: (b, si, 0)),
                  pl.BlockSpec((1, tm, RET_V_W), lambda b, si: (b, si, 0)),
                  pl.BlockSpec((1, tm, D), lambda b, si: (b, si, 1)),
                  pl.BlockSpec((1, tm, D), lambda b, si: (b, si, 2)),
                  pl.BlockSpec((1, N_ADA, D), lambda b, si: (b, 0, 0)),
                  pl.BlockSpec((1, D), lambda b, si: (0, 0)),
                  pl.BlockSpec((1, D), lambda b, si: (0, 0)),
                  const((MOBA_W, D), lambda b, si: (0, 0)),
                  const((RET_V_W, D), lambda b, si: (0, 0)),
                  const((D, D), lambda b, si: (0, 0)),
                  const((D, D_FF), lambda b, si: (0, 0)),
                  const((D_FF, D), lambda b, si: (0, 0))],
        out_specs=pl.BlockSpec((1, tm, D), lambda b, si: (b, si, 0)),
        out_shape=jax.ShapeDtypeStruct((B, S, D), F32),
        compiler_params=pltpu.CompilerParams(vmem_limit_bytes=VMEM_LIMIT),
        name="out_mlp",
    )(x, a_out, r_out, gates, gates, ada3, ln2_w.reshape(1, D), final_w.reshape(1, D),
      wmo, wro, wout, w1, w2)


def _rotary_tables(S, inv_freq, scale):
    pos = jnp.arange(S, dtype=F32)
    ang = pos[:, None] * inv_freq[None, :]
    cos, sin = jnp.cos(ang), jnp.sin(ang)
    zero = jnp.zeros_like(sin)
    reps = PAIR_W // HEAD_DIM
    cos_t = jnp.tile(jnp.concatenate([cos, cos], axis=1), (1, reps))
    sin_up = jnp.tile(jnp.concatenate([zero, sin], axis=1), (1, reps))
    sin_lo = jnp.tile(jnp.concatenate([-sin, zero], axis=1), (1, reps))
    return jnp.stack([cos_t, sin_up, sin_lo]) * scale


def kernel(x, c, ln1_w, ln2_w, w_ada, b_ada, w_in, ret_gn_w, w_moba_o, w_ret_o,
           w_out, w_ff1, w_ff2, final_norm_w):
    B, S, D = x.shape
    assert w_in.shape[0] == 1, "final norm is fused into the single layer's MLP kernel"
    moba_inv = 1.0 / (ROPE_THETA ** (jnp.arange(0, HEAD_DIM, 2, dtype=F32) / HEAD_DIM))
    ret_inv = 1.0 / (ROPE_THETA ** jnp.linspace(0.0, 1.0, HEAD_DIM // 2, dtype=F32))
    qk_scale = HEAD_DIM ** -0.5
    tabs = jnp.stack([_rotary_tables(S, moba_inv, qk_scale * LOG2E),
                      _rotary_tables(S, moba_inv, 1.0),
                      _rotary_tables(S, ret_inv, 1.0),
                      _rotary_tables(S, ret_inv, qk_scale)])

    ada3 = _ada(c, w_ada[0], b_ada[0]).reshape(B, N_ADA, D)
    qkv, gates = _in_proj(x, ada3, ln1_w[0], tabs, w_in[0].astype(BF16))
    a_out = _moba(qkv)
    r_out = _ret(qkv, gates, ret_gn_w[0])
    return _out_mlp(x, a_out, r_out, gates, ada3, ln2_w[0], final_norm_w,
                    w_moba_o[0].astype(BF16), w_ret_o[0].astype(BF16), w_out[0].astype(BF16),
                    w_ff1[0].astype(BF16), w_ff2[0].astype(BF16))
```

```python
import functools
import math

import jax
import jax.numpy as jnp
from jax import lax
from jax.experimental import pallas as pl
from jax.experimental.pallas import tpu as pltpu

F32 = jnp.float32
BF16 = jnp.bfloat16

D_MODEL = 1024
N_ADA = 6
HEAD_DIM = 64
HEADS = 8
PAIR_W = 2 * HEAD_DIM
N_PAIRS = HEADS // 2
MOBA_W = HEADS * HEAD_DIM
MOBA_BLOCK = 256
MOBA_TOPK = 3
MOBA_PAIRS_PER_STEP = 2
SCORE_LOOKAHEAD = 2
SUM_ROWS = 16
RET_V_DIM = 128
RET_V_W = HEADS * RET_V_DIM
RET_CHUNK = 256
D_FF = 4 * D_MODEL
FF_CHUNK = 1024
ROPE_THETA = 10000.0
EPS = 1e-6
LOG2E = math.log2(math.e)
QKV_W = 3 * MOBA_W + 2 * MOBA_W + RET_V_W
GATES_W = RET_V_W + 2 * D_MODEL
IN_WIDTH = QKV_W + GATES_W
PROJ_CHUNK = 512
TOKEN_TILE = 512
ROW_GROUP_EIGHTHS = (4, 8)
VMEM_LIMIT = 60 * 1024 * 1024

NT_DIMS = (((1,), (1,)), ((), ()))


def _sigmoid(t):
    return 1.0 / (1.0 + jnp.exp(-t))


def _rms_norm(x, w):
    return x * lax.rsqrt(jnp.mean(x * x, axis=-1, keepdims=True) + EPS) * w


def _ada_kernel(c_ref, w_ref, b_ref, o_ref):
    c = c_ref[...]
    ca = c * _sigmoid(c)
    o_ref[...] = jnp.dot(ca, w_ref[...], precision=lax.Precision.HIGHEST,
                         preferred_element_type=F32) + b_ref[...]


def _ada(c, w_ada, b_ada):
    B, D = c.shape
    N = w_ada.shape[1]
    bn = 1024
    return pl.pallas_call(
        _ada_kernel,
        grid=(N // bn,),
        in_specs=[pl.BlockSpec((B, D), lambda j: (0, 0)),
                  pl.BlockSpec((D, bn), lambda j: (0, j)),
                  pl.BlockSpec((1, bn), lambda j: (0, j))],
        out_specs=pl.BlockSpec((B, bn), lambda j: (0, j)),
        out_shape=jax.ShapeDtypeStruct((B, N), F32),
        name="ada",
    )(c, w_ada, b_ada.reshape(1, N))


def _rotary(tk, tab_ref, which, rows):
    return (tk * tab_ref[which, 0, rows, :]
            + pltpu.roll(tk, HEAD_DIM // 2, 1) * tab_ref[which, 1, rows, :]
            + pltpu.roll(tk, PAIR_W - HEAD_DIM // 2, 1) * tab_ref[which, 2, rows, :])


def _staggered(gens):
    gens, live = list(gens), []
    while gens or live:
        if gens:
            live.append(gens.pop(0))
        for gen in list(live):
            if next(gen, StopIteration) is StopIteration:
                live.remove(gen)


def _row_groups(tm):
    bounds = [0] + [tm * f // 8 for f in ROW_GROUP_EIGHTHS]
    return [slice(lo, hi) for lo, hi in zip(bounds[:-1], bounds[1:])]


def _in_proj_kernel(x_ref, ada_ref, ln_ref, tab_ref, w_ref, qkv_ref, gates_ref):
    shift = ada_ref[0, 0:1, :]
    scale = ada_ref[0, 1:2, :]
    rot = {0: 0, 1: 1, 3: 2, 4: 3}

    def stages(rows):
        h = (_rms_norm(x_ref[0, rows, :], ln_ref[...]) * (1.0 + scale) + shift).astype(BF16)
        yield
        for ci in range(IN_WIDTH // PROJ_CHUNK):
            c0 = ci * PROJ_CHUNK
            t = jnp.dot(h, w_ref[:, c0:c0 + PROJ_CHUNK], preferred_element_type=F32)
            if ci in rot:
                for i in range(PROJ_CHUNK // PAIR_W):
                    lo = i * PAIR_W
                    piece = _rotary(t[:, lo:lo + PAIR_W], tab_ref, rot[ci], rows)
                    qkv_ref[0, rows, c0 + lo:c0 + lo + PAIR_W] = piece.astype(BF16)
            elif c0 < QKV_W:
                qkv_ref[0, rows, c0:c0 + PROJ_CHUNK] = t.astype(BF16)
            else:
                gates_ref[0, rows, c0 - QKV_W:c0 - QKV_W + PROJ_CHUNK] = t.astype(BF16)
            yield

    _staggered(stages(rows) for rows in _row_groups(x_ref.shape[1]))


def _in_proj(x, ada3, ln1_w, tabs, w_in_bf16):
    B, S, D = x.shape
    tm = TOKEN_TILE
    return pl.pallas_call(
        _in_proj_kernel,
        grid=(S // tm, B),
        in_specs=[pl.BlockSpec((1, tm, D), lambda si, b: (b, si, 0)),
                  pl.BlockSpec((1, N_ADA, D), lambda si, b: (b, 0, 0)),
                  pl.BlockSpec((1, D), lambda si, b: (0, 0)),
                  pl.BlockSpec((4, 3, tm, PAIR_W), lambda si, b: (0, 0, si, 0)),
                  pl.BlockSpec((D, IN_WIDTH), lambda si, b: (0, 0),
                               pipeline_mode=pl.Buffered(1))],
        out_specs=[pl.BlockSpec((1, tm, QKV_W), lambda si, b: (b, si, 0)),
                   pl.BlockSpec((1, tm, GATES_W), lambda si, b: (b, si, 0))],
        out_shape=[jax.ShapeDtypeStruct((B, S, QKV_W), BF16),
                   jax.ShapeDtypeStruct((B, S, GATES_W), BF16)],
        compiler_params=pltpu.CompilerParams(vmem_limit_bytes=VMEM_LIMIT),
        name="in_proj",
    )(x, ada3, ln1_w.reshape(1, D), tabs, w_in_bf16)


def _moba_steps(q_ref, k_ref, v_ref, o_ref, vt_scr):
    S = q_ref.shape[1]
    T = MOBA_BLOCK
    nb = S // T
    lane = lax.broadcasted_iota(jnp.int32, (1, PAIR_W), 1)
    head_masks = [(lane >= h * HEAD_DIM) & (lane < (h + 1) * HEAD_DIM) for h in range(2)]
    krow = lax.broadcasted_iota(jnp.int32, (T, 2 * T), 0)
    qcol = lax.broadcasted_iota(jnp.int32, (T, 2 * T), 1) % T
    causal2 = krow <= qcol

    def score_steps(j, st):
        qt = q_ref[0, j * T:(j + 1) * T, :]
        zero = jnp.zeros_like(qt)
        q2 = jnp.concatenate([jnp.where(hm, qt, zero) for hm in head_masks], axis=0)
        st["blocks"], st["cmax"] = [], []
        for n in range(j + 1):
            def step(n=n):
                sn = lax.dot_general(k_ref[0, n * T:(n + 1) * T, :], q2, NT_DIMS,
                                     preferred_element_type=F32)
                if n == j:
                    sn = jnp.where(causal2, sn, -jnp.inf)
                st["blocks"].append(sn)
                st["cmax"].append(jnp.max(sn, axis=0, keepdims=True))
            yield step

    biases = []

    def prologue_steps():
        for n in range(nb):
            vt = v_ref[0, n * T:(n + 1) * T, :].T
            for h in range(2):
                vt_scr[h, n, 0:HEAD_DIM, :] = vt[h * HEAD_DIM:(h + 1) * HEAD_DIM]
                vt_scr[h, n, HEAD_DIM:, :] = jnp.ones((SUM_ROWS, T), BF16)
            if n % 2:
                yield
        kmean = jnp.mean(k_ref[0].astype(F32).reshape(nb, T, PAIR_W), axis=1)
        km_rows = []
        for hm in head_masks:
            kmh = jnp.where(hm, kmean, 0.0)
            hi = kmh.astype(BF16).astype(F32)
            mid = (kmh - hi).astype(BF16).astype(F32)
            km_rows += [hi, mid, kmh - hi - mid]
        km = jnp.concatenate(km_rows, axis=0).astype(BF16)
        gate_terms = lax.dot_general(km, q_ref[0], NT_DIMS,
                                     preferred_element_type=F32)
        yield
        blk = lax.broadcasted_iota(jnp.int32, (nb, S), 0)
        qblk = lax.broadcasted_iota(jnp.int32, (nb, S), 1) // T
        past = blk < qblk
        for h in range(2):
            gt = gate_terms[3 * nb * h:3 * nb * (h + 1)]
            gate = gt[0:nb] + gt[nb:2 * nb] + gt[2 * nb:3 * nb]
            g = jnp.where(past, gate, -jnp.inf)
            rank = jnp.zeros((nb, S), jnp.int32)
            for m in range(nb - 1):
                gm = g[m:m + 1, :]
                beats = (gm > g) | ((gm == g) & (m < blk))
                rank = rank + beats.astype(jnp.int32)
            sel = past & (rank < MOBA_TOPK)
            biases.append(jnp.where(sel, 0.0, -jnp.inf).astype(F32))
            yield

    order = list(range(nb - 1, -1, -1))
    states = [dict() for _ in range(nb)]
    first = [step for j in order[:SCORE_LOOKAHEAD] for step in score_steps(j, states[j])]
    pro = prologue_steps()
    for step in first:
        step()
        next(pro, None)
        yield
    for _ in pro:
        yield

    def value_steps(j, st):
        bias = jnp.concatenate([b[:, j * T:(j + 1) * T] for b in biases], axis=1)
        m = st["cmax"][j]
        for n in range(j):
            m = jnp.maximum(m, st["cmax"][n] + bias[n:n + 1])
        acc = [jnp.zeros((HEAD_DIM + SUM_ROWS, T), F32) for _ in range(2)]
        for n in range(j + 1):
            def step(n=n):
                off = m if n == j else m - bias[n:n + 1]
                p = jnp.exp2((st["blocks"][n] - off).astype(BF16))
                for h in range(2):
                    acc[h] = acc[h] + jnp.dot(vt_scr[h, n], p[:, h * T:(h + 1) * T],
                                              preferred_element_type=F32)
            yield step

        def finish():
            outs = [a[0:HEAD_DIM] * (1.0 / a[HEAD_DIM:HEAD_DIM + 1]) for a in acc]
            o_ref[0, j * T:(j + 1) * T, :] = jnp.concatenate(outs, axis=0).T.astype(BF16)
        yield finish

    for i, j in enumerate(order):
        ahead = i + SCORE_LOOKAHEAD
        nxt = list(score_steps(order[ahead], states[order[ahead]])) if ahead < nb else []
        cur = list(value_steps(j, states[j]))
        for t in range(max(len(nxt), len(cur))):
            if t < len(nxt):
                nxt[t]()
            if t < len(cur):
                cur[t]()
            yield
        states[j].clear()


def _ret_steps(q_ref, k_ref, v_ref, g_ref, gn_ref, dec_ref, zeta_ref, xi_ref, cdec_ref, o_ref):
    S = q_ref.shape[1]
    C = RET_CHUNK
    nc = S // C
    lane = lax.broadcasted_iota(jnp.int32, (1, PAIR_W), 1)
    state = [None, None]
    gains = [gn_ref[:, h * RET_V_DIM:(h + 1) * RET_V_DIM] * math.sqrt(RET_V_DIM) for h in range(2)]
    for n in range(nc):
        rows = slice(n * C, (n + 1) * C)
        q = q_ref[0, rows, :]
        kc = k_ref[0, rows, :]
        kt = kc.T
        for h in range(2):
            hm = (lane >= h * HEAD_DIM) & (lane < (h + 1) * HEAD_DIM)
            cols = slice(h * RET_V_DIM, (h + 1) * RET_V_DIM)
            qh = jnp.where(hm, q, jnp.zeros_like(q))
            vh = v_ref[0, rows, cols]
            s = lax.dot_general(qh, kc, NT_DIMS, preferred_element_type=F32)
            p = s.astype(BF16) * dec_ref[h]
            o = jnp.dot(p, vh, preferred_element_type=F32)
            if n > 0:
                o = o + jnp.dot(qh, state[h].astype(BF16), preferred_element_type=F32) * xi_ref[h]
            if n < nc - 1:
                kz = kt * zeta_ref[h]
                kv = jnp.dot(kz, vh, preferred_element_type=F32)
                state[h] = kv if n == 0 else state[h] * cdec_ref[h] + kv
            yield
            d = o - jnp.sum(o, axis=-1, keepdims=True) * (1.0 / RET_V_DIM)
            r = lax.rsqrt(jnp.sum(d * d, axis=-1, keepdims=True) + RET_V_DIM * EPS)
            y = d * r * gains[h]
            hg = 0.5 * g_ref[0, rows, cols].astype(F32)
            o_ref[0, rows, cols] = (hg * (1.0 + jnp.tanh(hg)) * y).astype(BF16)
            yield


def _moba_kernel(q_ref, k_ref, v_ref, o_ref, vt_scr):
    for i in range(MOBA_PAIRS_PER_STEP):
        lanes = pl.ds(i * PAIR_W, PAIR_W)
        views = [r.at[:, :, lanes] for r in (q_ref, k_ref, v_ref, o_ref)]
        for _ in _moba_steps(*views, vt_scr.at[i]):
            pass


def _ret_kernel(*refs):
    for _ in _ret_steps(*refs):
        pass


def _ret_tables():
    C = RET_CHUNK
    log_g = jnp.log(1.0 - jnp.power(2.0, -5.0 - jnp.arange(HEADS, dtype=F32)))
    i = jnp.arange(C, dtype=F32)
    diff = i[:, None] - i[None, :]
    dec = jnp.where(diff >= 0, jnp.exp(jnp.maximum(diff, 0.0)[None] * log_g[:, None, None]), 0.0)
    zeta = jnp.exp((C - 1 - i)[None, :] * log_g[:, None])[:, None, :]
    xi = jnp.exp((i + 1)[None, :] * log_g[:, None])
    xi = jnp.broadcast_to(xi[:, :, None], (HEADS, C, RET_V_DIM))
    cdec = jnp.broadcast_to(jnp.exp(C * log_g)[:, None, None], (HEADS, 1, RET_V_DIM))
    return dec.astype(BF16), zeta.astype(BF16), xi, cdec


def _pair_block(col_off):
    return lambda b, p: (b, 0, col_off // PAIR_W + p)


def _moba(qkv):
    B, S, _ = qkv.shape
    nb = S // MOBA_BLOCK
    w = MOBA_PAIRS_PER_STEP * PAIR_W

    def slab(col_off):
        return lambda b, p: (b, 0, col_off // w + p)

    return pl.pallas_call(
        _moba_kernel,
        grid=(B, N_PAIRS // MOBA_PAIRS_PER_STEP),
        in_specs=[pl.BlockSpec((1, S, w), slab(0)),
                  pl.BlockSpec((1, S, w), slab(MOBA_W)),
                  pl.BlockSpec((1, S, w), slab(2 * MOBA_W))],
        out_specs=pl.BlockSpec((1, S, w), slab(0)),
        out_shape=jax.ShapeDtypeStruct((B, S, MOBA_W), BF16),
        scratch_shapes=[pltpu.VMEM((MOBA_PAIRS_PER_STEP, 2, nb, HEAD_DIM + SUM_ROWS, MOBA_BLOCK),
                                   BF16)],
        compiler_params=pltpu.CompilerParams(vmem_limit_bytes=VMEM_LIMIT),
        name="moba",
    )(qkv, qkv, qkv)


def _ret(qkv, gates, gn_w):
    B, S, _ = qkv.shape
    C = RET_CHUNK
    v_blk = 5 * MOBA_W // (2 * RET_V_DIM)
    dec, zeta, xi, cdec = _ret_tables()
    return pl.pallas_call(
        _ret_kernel,
        grid=(B, N_PAIRS),
        in_specs=[pl.BlockSpec((1, S, PAIR_W), _pair_block(3 * MOBA_W)),
                  pl.BlockSpec((1, S, PAIR_W), _pair_block(4 * MOBA_W)),
                  pl.BlockSpec((1, S, 2 * RET_V_DIM), lambda b, p: (b, 0, v_blk + p)),
                  pl.BlockSpec((1, S, 2 * RET_V_DIM), lambda b, p: (b, 0, p)),
                  pl.BlockSpec((1, 2 * RET_V_DIM), lambda b, p: (0, p)),
                  pl.BlockSpec((2, C, C), lambda b, p: (p, 0, 0)),
                  pl.BlockSpec((2, 1, C), lambda b, p: (p, 0, 0)),
                  pl.BlockSpec((2, C, RET_V_DIM), lambda b, p: (p, 0, 0)),
                  pl.BlockSpec((2, 1, RET_V_DIM), lambda b, p: (p, 0, 0))],
        out_specs=pl.BlockSpec((1, S, 2 * RET_V_DIM), lambda b, p: (b, 0, p)),
        out_shape=jax.ShapeDtypeStruct((B, S, RET_V_W), BF16),
        name="ret",
    )(qkv, qkv, qkv, gates, gn_w.reshape(1, RET_V_W), dec, zeta, xi, cdec)


def _out_mlp_kernel(x_ref, a_ref, r_ref, ga_ref, gr_ref, ada_ref, ln2_ref, fw_ref,
                    wmo_ref, wro_ref, wout_ref, w1_ref, w2_ref, o_ref):
    gate1 = ada_ref[0, 2:3, :]
    shift2 = ada_ref[0, 3:4, :]
    scale2 = ada_ref[0, 4:5, :]
    gate2 = ada_ref[0, 5:6, :]

    def stages(rows):
        ya = jnp.dot(a_ref[0, rows, :], wmo_ref[...], preferred_element_type=F32)
        yr = jnp.dot(r_ref[0, rows, :], wro_ref[...], preferred_element_type=F32)
        yield
        merged = (_sigmoid(ga_ref[0, rows, :].astype(F32)) * ya
                  + _sigmoid(gr_ref[0, rows, :].astype(F32)) * yr).astype(BF16)
        yield
        mo = jnp.dot(merged, wout_ref[...], preferred_element_type=F32)
        yield
        x1 = x_ref[0, rows, :] + gate1 * mo
        h2 = (_rms_norm(x1, ln2_ref[...]) * (1.0 + scale2) + shift2).astype(BF16)
        yield
        acc = jnp.zeros_like(x1)
        for ci in range(D_FF // FF_CHUNK):
            c0 = ci * FF_CHUNK
            hid = jnp.dot(h2, w1_ref[:, c0:c0 + FF_CHUNK], preferred_element_type=F32)
            yield
            hid = jnp.square(jnp.maximum(hid, 0.0)).astype(BF16)
            yield
            acc = acc + jnp.dot(hid, w2_ref[c0:c0 + FF_CHUNK, :], preferred_element_type=F32)
            yield
        x2 = x1 + gate2 * acc
        o_ref[0, rows, :] = _rms_norm(x2, fw_ref[...])

    _staggered(stages(rows) for rows in _row_groups(x_ref.shape[1]))


def _out_mlp(x, a_out, r_out, gates, ada3, ln2_w, final_w, wmo, wro, wout, w1, w2):
    B, S, D = x.shape
    tm = TOKEN_TILE
    const = functools.partial(pl.BlockSpec, pipeline_mode=pl.Buffered(1))
    return pl.pallas_call(
        _out_mlp_kernel,
        grid=(B, S // tm),
        in_specs=[pl.BlockSpec((1, tm, D), lambda b, si: (b, si, 0)),
                  pl.BlockSpec((1, tm, MOBA_W), lambda b, si: (b, si, 0)),
                  pl.BlockSpec((1, tm, RET_V_W), lambda b, si: (b, si, 0)),
                  pl.BlockSpec((1, tm, D), lambda b, si: (b, si, 1)),
                  pl.BlockSpec((1, tm, D), lambda b, si: (b, si, 2)),
                  pl.BlockSpec((1, N_ADA, D), lambda b, si: (b, 0, 0)),
                  pl.BlockSpec((1, D), lambda b, si: (0, 0)),
                  pl.BlockSpec((1, D), lambda b, si: (0, 0)),
                  const((MOBA_W, D), lambda b, si: (0, 0)),
                  const((RET_V_W, D), lambda b, si: (0, 0)),
                  const((D, D), lambda b, si: (0, 0)),
                  const((D, D_FF), lambda b, si: (0, 0)),
                  const((D_FF, D), lambda b, si: (0, 0))],
        out_specs=pl.BlockSpec((1, tm, D), lambda b, si: (b, si, 0)),
        out_shape=jax.ShapeDtypeStruct((B, S, D), F32),
        compiler_params=pltpu.CompilerParams(vmem_limit_bytes=VMEM_LIMIT),
        name="out_mlp",
    )(x, a_out, r_out, gates, gates, ada3, ln2_w.reshape(1, D), final_w.reshape(1, D),
      wmo, wro, wout, w1, w2)


def _rotary_tables(S, inv_freq, scale):
    pos = jnp.arange(S, dtype=F32)
    ang = pos[:, None] * inv_freq[None, :]
    cos, sin = jnp.cos(ang), jnp.sin(ang)
    zero = jnp.zeros_like(sin)
    reps = PAIR_W // HEAD_DIM
    cos_t = jnp.tile(jnp.concatenate([cos, cos], axis=1), (1, reps))
    sin_up = jnp.tile(jnp.concatenate([zero, sin], axis=1), (1, reps))
    sin_lo = jnp.tile(jnp.concatenate([-sin, zero], axis=1), (1, reps))
    return jnp.stack([cos_t, sin_up, sin_lo]) * scale


def kernel(x, c, ln1_w, ln2_w, w_ada, b_ada, w_in, ret_gn_w, w_moba_o, w_ret_o,
           w_out, w_ff1, w_ff2, final_norm_w):
    B, S, D = x.shape
    assert w_in.shape[0] == 1, "final norm is fused into the single layer's MLP kernel"
    moba_inv = 1.0 / (ROPE_THETA ** (jnp.arange(0, HEAD_DIM, 2, dtype=F32) / HEAD_DIM))
    ret_inv = 1.0 / (ROPE_THETA ** jnp.linspace(0.0, 1.0, HEAD_DIM // 2, dtype=F32))
    qk_scale = HEAD_DIM ** -0.5
    tabs = jnp.stack([_rotary_tables(S, moba_inv, qk_scale * LOG2E),
                      _rotary_tables(S, moba_inv, 1.0),
                      _rotary_tables(S, ret_inv, 1.0),
                      _rotary_tables(S, ret_inv, qk_scale)])

    ada3 = _ada(c, w_ada[0], b_ada[0]).reshape(B, N_ADA, D)
    qkv, gates = _in_proj(x, ada3, ln1_w[0], tabs, w_in[0].astype(BF16))
    a_out = _moba(qkv)
    r_out = _ret(qkv, gates, ret_gn_w[0])
    return _out_mlp(x, a_out, r_out, gates, ada3, ln2_w[0], final_norm_w,
                    w_moba_o[0].astype(BF16), w_ret_o[0].astype(BF16), w_out[0].astype(BF16),
                    w_ff1[0].astype(BF16), w_ff2[0].astype(BF16))
```

```python
import functools
import math

import jax
import jax.numpy as jnp
from jax import lax
from jax.experimental import pallas as pl
from jax.experimental.pallas import tpu as pltpu

F32 = jnp.float32
BF16 = jnp.bfloat16

D_MODEL = 1024
N_ADA = 6
HEAD_DIM = 64
HEADS = 8
PAIR_W = 2 * HEAD_DIM
N_PAIRS = HEADS // 2
MOBA_W = HEADS * HEAD_DIM
MOBA_BLOCK = 256
MOBA_TOPK = 3
SCORE_LOOKAHEAD = 2
SUM_ROWS = 16
RET_V_DIM = 128
RET_V_W = HEADS * RET_V_DIM
RET_CHUNK = 256
D_FF = 4 * D_MODEL
FF_CHUNK = 1024
ROPE_THETA = 10000.0
EPS = 1e-6
LOG2E = math.log2(math.e)
QKV_W = 3 * MOBA_W + 2 * MOBA_W + RET_V_W
GATES_W = RET_V_W + 2 * D_MODEL
IN_WIDTH = QKV_W + GATES_W
PROJ_CHUNK = 512
TOKEN_TILE = 512
ROW_GROUP_EIGHTHS = (4, 8)
VMEM_LIMIT = 60 * 1024 * 1024

NT_DIMS = (((1,), (1,)), ((), ()))


def _sigmoid(t):
    return 1.0 / (1.0 + jnp.exp(-t))


def _rms_norm(x, w):
    return x * lax.rsqrt(jnp.mean(x * x, axis=-1, keepdims=True) + EPS) * w


def _ada_kernel(c_ref, w_ref, b_ref, o_ref):
    c = c_ref[...]
    ca = c * _sigmoid(c)
    o_ref[...] = jnp.dot(ca, w_ref[...], precision=lax.Precision.HIGHEST,
                         preferred_element_type=F32) + b_ref[...]


def _ada(c, w_ada, b_ada):
    B, D = c.shape
    N = w_ada.shape[1]
    bn = 1024
    return pl.pallas_call(
        _ada_kernel,
        grid=(N // bn,),
        in_specs=[pl.BlockSpec((B, D), lambda j: (0, 0)),
                  pl.BlockSpec((D, bn), lambda j: (0, j)),
                  pl.BlockSpec((1, bn), lambda j: (0, j))],
        out_specs=pl.BlockSpec((B, bn), lambda j: (0, j)),
        out_shape=jax.ShapeDtypeStruct((B, N), F32),
        name="ada",
    )(c, w_ada, b_ada.reshape(1, N))


def _rotary(tk, tab_ref, which, rows):
    return (tk * tab_ref[which, 0, rows, :]
            + pltpu.roll(tk, HEAD_DIM // 2, 1) * tab_ref[which, 1, rows, :]
            + pltpu.roll(tk, PAIR_W - HEAD_DIM // 2, 1) * tab_ref[which, 2, rows, :])


def _staggered(gens):
    gens, live = list(gens), []
    while gens or live:
        if gens:
            live.append(gens.pop(0))
        for gen in list(live):
            if next(gen, StopIteration) is StopIteration:
                live.remove(gen)


def _row_groups(tm):
    bounds = [0] + [tm * f // 8 for f in ROW_GROUP_EIGHTHS]
    return [slice(lo, hi) for lo, hi in zip(bounds[:-1], bounds[1:])]


def _in_proj_kernel(x_ref, ada_ref, ln_ref, tab_ref, w_ref, qkv_ref, gates_ref):
    shift = ada_ref[0, 0:1, :]
    scale = ada_ref[0, 1:2, :]
    rot = {0: 0, 1: 1, 3: 2, 4: 3}

    def stages(rows):
        h = (_rms_norm(x_ref[0, rows, :], ln_ref[...]) * (1.0 + scale) + shift).astype(BF16)
        yield
        for ci in range(IN_WIDTH // PROJ_CHUNK):
            c0 = ci * PROJ_CHUNK
            t = jnp.dot(h, w_ref[:, c0:c0 + PROJ_CHUNK], preferred_element_type=F32)
            if ci in rot:
                for i in range(PROJ_CHUNK // PAIR_W):
                    lo = i * PAIR_W
                    piece = _rotary(t[:, lo:lo + PAIR_W], tab_ref, rot[ci], rows)
                    qkv_ref[0, rows, c0 + lo:c0 + lo + PAIR_W] = piece.astype(BF16)
            elif c0 < QKV_W:
                qkv_ref[0, rows, c0:c0 + PROJ_CHUNK] = t.astype(BF16)
            else:
                gates_ref[0, rows, c0 - QKV_W:c0 - QKV_W + PROJ_CHUNK] = t.astype(BF16)
            yield

    _staggered(stages(rows) for rows in _row_groups(x_ref.shape[1]))


def _in_proj(x, ada3, ln1_w, tabs, w_in_bf16):
    B, S, D = x.shape
    tm = TOKEN_TILE
    return pl.pallas_call(
        _in_proj_kernel,
        grid=(S // tm, B),
        in_specs=[pl.BlockSpec((1, tm, D), lambda si, b: (b, si, 0)),
                  pl.BlockSpec((1, N_ADA, D), lambda si, b: (b, 0, 0)),
                  pl.BlockSpec((1, D), lambda si, b: (0, 0)),
                  pl.BlockSpec((4, 3, tm, PAIR_W), lambda si, b: (0, 0, si, 0)),
                  pl.BlockSpec((D, IN_WIDTH), lambda si, b: (0, 0),
                               pipeline_mode=pl.Buffered(1))],
        out_specs=[pl.BlockSpec((1, tm, QKV_W), lambda si, b: (b, si, 0)),
                   pl.BlockSpec((1, tm, GATES_W), lambda si, b: (b, si, 0))],
        out_shape=[jax.ShapeDtypeStruct((B, S, QKV_W), BF16),
                   jax.ShapeDtypeStruct((B, S, GATES_W), BF16)],
        compiler_params=pltpu.CompilerParams(vmem_limit_bytes=VMEM_LIMIT),
        name="in_proj",
    )(x, ada3, ln1_w.reshape(1, D), tabs, w_in_bf16)


def _moba_steps(q_ref, k_ref, v_ref, o_ref, vt_scr):
    S = q_ref.shape[1]
    T = MOBA_BLOCK
    nb = S // T
    lane = lax.broadcasted_iota(jnp.int32, (1, PAIR_W), 1)
    head_masks = [(lane >= h * HEAD_DIM) & (lane < (h + 1) * HEAD_DIM) for h in range(2)]
    krow = lax.broadcasted_iota(jnp.int32, (T, 2 * T), 0)
    qcol = lax.broadcasted_iota(jnp.int32, (T, 2 * T), 1) % T
    causal2 = krow <= qcol

    def score_steps(j, st):
        qt = q_ref[0, j * T:(j + 1) * T, :]
        zero = jnp.zeros_like(qt)
        q2 = jnp.concatenate([jnp.where(hm, qt, zero) for hm in head_masks], axis=0)
        st["blocks"], st["cmax"] = [], []
        for n in range(j + 1):
            def step(n=n):
                sn = lax.dot_general(k_ref[0, n * T:(n + 1) * T, :], q2, NT_DIMS,
                                     preferred_element_type=F32)
                if n == j:
                    sn = jnp.where(causal2, sn, -jnp.inf)
                st["blocks"].append(sn)
                st["cmax"].append(jnp.max(sn, axis=0, keepdims=True))
            yield step

    biases = []

    def prologue_steps():
        for n in range(nb):
            vt = v_ref[0, n * T:(n + 1) * T, :].T
            for h in range(2):
                vt_scr[h, n, 0:HEAD_DIM, :] = vt[h * HEAD_DIM:(h + 1) * HEAD_DIM]
                vt_scr[h, n, HEAD_DIM:, :] = jnp.ones((SUM_ROWS, T), BF16)
            if n % 2:
                yield
        kmean = jnp.mean(k_ref[0].astype(F32).reshape(nb, T, PAIR_W), axis=1)
        km_rows = []
        for hm in head_masks:
            kmh = jnp.where(hm, kmean, 0.0)
            hi = kmh.astype(BF16).astype(F32)
            mid = (kmh - hi).astype(BF16).astype(F32)
            km_rows += [hi, mid, kmh - hi - mid]
        km = jnp.concatenate(km_rows, axis=0).astype(BF16)
        gate_terms = lax.dot_general(km, q_ref[0], NT_DIMS,
                                     preferred_element_type=F32)
        yield
        blk = lax.broadcasted_iota(jnp.int32, (nb, S), 0)
        qblk = lax.broadcasted_iota(jnp.int32, (nb, S), 1) // T
        past = blk < qblk
        for h in range(2):
            gt = gate_terms[3 * nb * h:3 * nb * (h + 1)]
            gate = gt[0:nb] + gt[nb:2 * nb] + gt[2 * nb:3 * nb]
            g = jnp.where(past, gate, -jnp.inf)
            rank = jnp.zeros((nb, S), jnp.int32)
            for m in range(nb - 1):
                gm = g[m:m + 1, :]
                beats = (gm > g) | ((gm == g) & (m < blk))
                rank = rank + beats.astype(jnp.int32)
            sel = past & (rank < MOBA_TOPK)
            biases.append(jnp.where(sel, 0.0, -jnp.inf).astype(F32))
            yield

    order = list(range(nb - 1, -1, -1))
    states = [dict() for _ in range(nb)]
    first = [step for j in order[:SCORE_LOOKAHEAD] for step in score_steps(j, states[j])]
    pro = prologue_steps()
    for step in first:
        step()
        next(pro, None)
        yield
    for _ in pro:
        yield

    def value_steps(j, st):
        bias = jnp.concatenate([b[:, j * T:(j + 1) * T] for b in biases], axis=1)
        m = st["cmax"][j]
        for n in range(j):
            m = jnp.maximum(m, st["cmax"][n] + bias[n:n + 1])
        acc = [jnp.zeros((HEAD_DIM + SUM_ROWS, T), F32) for _ in range(2)]
        for n in range(j + 1):
            def step(n=n):
                off = m if n == j else m - bias[n:n + 1]
                p = jnp.exp2((st["blocks"][n] - off).astype(BF16))
                for h in range(2):
                    acc[h] = acc[h] + jnp.dot(vt_scr[h, n], p[:, h * T:(h + 1) * T],
                                              preferred_element_type=F32)
            yield step

        def finish():
            outs = [a[0:HEAD_DIM] * (1.0 / a[HEAD_DIM:HEAD_DIM + 1]) for a in acc]
            o_ref[0, j * T:(j + 1) * T, :] = jnp.concatenate(outs, axis=0).T.astype(BF16)
        yield finish

    for i, j in enumerate(order):
        ahead = i + SCORE_LOOKAHEAD
        nxt = list(score_steps(order[ahead], states[order[ahead]])) if ahead < nb else []
        cur = list(value_steps(j, states[j]))
        for t in range(max(len(nxt), len(cur))):
            if t < len(nxt):
                nxt[t]()
            if t < len(cur):
                cur[t]()
            yield
        states[j].clear()


def _ret_steps(q_ref, k_ref, v_ref, g_ref, gn_ref, dec_ref, zeta_ref, xi_ref, cdec_ref, o_ref):
    S = q_ref.shape[1]
    C = RET_CHUNK
    nc = S // C
    lane = lax.broadcasted_iota(jnp.int32, (1, PAIR_W), 1)
    state = [None, None]
    gains = [gn_ref[:, h * RET_V_DIM:(h + 1) * RET_V_DIM] * math.sqrt(RET_V_DIM) for h in range(2)]
    for n in range(nc):
        rows = slice(n * C, (n + 1) * C)
        q = q_ref[0, rows, :]
        kc = k_ref[0, rows, :]
        kt = kc.T
        for h in range(2):
            hm = (lane >= h * HEAD_DIM) & (lane < (h + 1) * HEAD_DIM)
            cols = slice(h * RET_V_DIM, (h + 1) * RET_V_DIM)
            qh = jnp.where(hm, q, jnp.zeros_like(q))
            vh = v_ref[0, rows, cols]
            s = lax.dot_general(qh, kc, NT_DIMS, preferred_element_type=F32)
            p = s.astype(BF16) * dec_ref[h]
            o = jnp.dot(p, vh, preferred_element_type=F32)
            if n > 0:
                o = o + jnp.dot(qh, state[h].astype(BF16), preferred_element_type=F32) * xi_ref[h]
            if n < nc - 1:
                kz = kt * zeta_ref[h]
                kv = jnp.dot(kz, vh, preferred_element_type=F32)
                state[h] = kv if n == 0 else state[h] * cdec_ref[h] + kv
            yield
            d = o - jnp.sum(o, axis=-1, keepdims=True) * (1.0 / RET_V_DIM)
            r = lax.rsqrt(jnp.sum(d * d, axis=-1, keepdims=True) + RET_V_DIM * EPS)
            y = d * r * gains[h]
            hg = 0.5 * g_ref[0, rows, cols].astype(F32)
            o_ref[0, rows, cols] = (hg * (1.0 + jnp.tanh(hg)) * y).astype(BF16)
            yield


def _moba_kernel(*refs):
    for _ in _moba_steps(*refs):
        pass


def _ret_kernel(*refs):
    for _ in _ret_steps(*refs):
        pass


def _ret_tables():
    C = RET_CHUNK
    log_g = jnp.log(1.0 - jnp.power(2.0, -5.0 - jnp.arange(HEADS, dtype=F32)))
    i = jnp.arange(C, dtype=F32)
    diff = i[:, None] - i[None, :]
    dec = jnp.where(diff >= 0, jnp.exp(jnp.maximum(diff, 0.0)[None] * log_g[:, None, None]), 0.0)
    zeta = jnp.exp((C - 1 - i)[None, :] * log_g[:, None])[:, None, :]
    xi = jnp.exp((i + 1)[None, :] * log_g[:, None])
    xi = jnp.broadcast_to(xi[:, :, None], (HEADS, C, RET_V_DIM))
    cdec = jnp.broadcast_to(jnp.exp(C * log_g)[:, None, None], (HEADS, 1, RET_V_DIM))
    return dec.astype(BF16), zeta.astype(BF16), xi, cdec


def _pair_block(col_off):
    return lambda b, p: (b, 0, col_off // PAIR_W + p)


def _moba(qkv):
    B, S, _ = qkv.shape
    nb = S // MOBA_BLOCK
    return pl.pallas_call(
        _moba_kernel,
        grid=(B, N_PAIRS),
        in_specs=[pl.BlockSpec((1, S, PAIR_W), _pair_block(0)),
                  pl.BlockSpec((1, S, PAIR_W), _pair_block(MOBA_W)),
                  pl.BlockSpec((1, S, PAIR_W), _pair_block(2 * MOBA_W))],
        out_specs=pl.BlockSpec((1, S, PAIR_W), _pair_block(0)),
        out_shape=jax.ShapeDtypeStruct((B, S, MOBA_W), BF16),
        scratch_shapes=[pltpu.VMEM((2, nb, HEAD_DIM + SUM_ROWS, MOBA_BLOCK), BF16)],
        name="moba",
    )(qkv, qkv, qkv)


def _ret(qkv, gates, gn_w):
    B, S, _ = qkv.shape
    C = RET_CHUNK
    v_blk = 5 * MOBA_W // (2 * RET_V_DIM)
    dec, zeta, xi, cdec = _ret_tables()
    return pl.pallas_call(
        _ret_kernel,
        grid=(B, N_PAIRS),
        in_specs=[pl.BlockSpec((1, S, PAIR_W), _pair_block(3 * MOBA_W)),
                  pl.BlockSpec((1, S, PAIR_W), _pair_block(4 * MOBA_W)),
                  pl.BlockSpec((1, S, 2 * RET_V_DIM), lambda b, p: (b, 0, v_blk + p)),
                  pl.BlockSpec((1, S, 2 * RET_V_DIM), lambda b, p: (b, 0, p)),
                  pl.BlockSpec((1, 2 * RET_V_DIM), lambda b, p: (0, p)),
                  pl.BlockSpec((2, C, C), lambda b, p: (p, 0, 0)),
                  pl.BlockSpec((2, 1, C), lambda b, p: (p, 0, 0)),
                  pl.BlockSpec((2, C, RET_V_DIM), lambda b, p: (p, 0, 0)),
                  pl.BlockSpec((2, 1, RET_V_DIM), lambda b, p: (p, 0, 0))],
        out_specs=pl.BlockSpec((1, S, 2 * RET_V_DIM), lambda b, p: (b, 0, p)),
        out_shape=jax.ShapeDtypeStruct((B, S, RET_V_W), BF16),
        name="ret",
    )(qkv, qkv, qkv, gates, gn_w.reshape(1, RET_V_W), dec, zeta, xi, cdec)


def _out_mlp_kernel(x_ref, a_ref, r_ref, ga_ref, gr_ref, ada_ref, ln2_ref, fw_ref,
                    wmo_ref, wro_ref, wout_ref, w1_ref, w2_ref, o_ref):
    gate1 = ada_ref[0, 2:3, :]
    shift2 = ada_ref[0, 3:4, :]
    scale2 = ada_ref[0, 4:5, :]
    gate2 = ada_ref[0, 5:6, :]

    def stages(rows):
        ya = jnp.dot(a_ref[0, rows, :], wmo_ref[...], preferred_element_type=F32)
        yr = jnp.dot(r_ref[0, rows, :], wro_ref[...], preferred_element_type=F32)
        yield
        merged = (_sigmoid(ga_ref[0, rows, :].astype(F32)) * ya
                  + _sigmoid(gr_ref[0, rows, :].astype(F32)) * yr).astype(BF16)
        yield
        mo = jnp.dot(merged, wout_ref[...], preferred_element_type=F32)
        yield
        x1 = x_ref[0, rows, :] + gate1 * mo
        h2 = (_rms_norm(x1, ln2_ref[...]) * (1.0 + scale2) + shift2).astype(BF16)
        yield
        acc = jnp.zeros_like(x1)
        for ci in range(D_FF // FF_CHUNK):
            c0 = ci * FF_CHUNK
            hid = jnp.dot(h2, w1_ref[:, c0:c0 + FF_CHUNK], preferred_element_type=F32)
            yield
            hid = jnp.square(jnp.maximum(hid, 0.0)).astype(BF16)
            yield
            acc = acc + jnp.dot(hid, w2_ref[c0:c0 + FF_CHUNK, :], preferred_element_type=F32)
            yield
        x2 = x1 + gate2 * acc
        o_ref[0, rows, :] = _rms_norm(x2, fw_ref[...])

    _staggered(stages(rows) for rows in _row_groups(x_ref.shape[1]))


def _out_mlp(x, a_out, r_out, gates, ada3, ln2_w, final_w, wmo, wro, wout, w1, w2):
    B, S, D = x.shape
    tm = TOKEN_TILE
    const = functools.partial(pl.BlockSpec, pipeline_mode=pl.Buffered(1))
    return pl.pallas_call(
        _out_mlp_kernel,
        grid=(B, S // tm),
        in_specs=[pl.BlockSpec((1, tm, D), lambda b, si: (b, si, 0)),
                  pl.BlockSpec((1, tm, MOBA_W), lambda b, si: (b, si, 0)),
                  pl.BlockSpec((1, tm, RET_V_W), lambda b, si: (b, si, 0)),
                  pl.BlockSpec((1, tm, D), lambda b, si: (b, si, 1)),
                  pl.BlockSpec((1, tm, D), lambda b, si: (b, si, 2)),
                  pl.BlockSpec((1, N_ADA, D), lambda b, si: (b, 0, 0)),
                  pl.BlockSpec((1, D), lambda b, si: (0, 0)),
                  pl.BlockSpec((1, D), lambda b, si: (0, 0)),
                  const((MOBA_W, D), lambda b, si: (0, 0)),
                  const((RET_V_W, D), lambda b, si: (0, 0)),
                  const((D, D), lambda b, si: (0, 0)),
                  const((D, D_FF), lambda b, si: (0, 0)),
                  const((D_FF, D), lambda b, si: (0, 0))],
        out_specs=pl.BlockSpec((1, tm, D), lambda b, si: (b, si, 0)),
        out_shape=jax.ShapeDtypeStruct((B, S, D), F32),
        compiler_params=pltpu.CompilerParams(vmem_limit_bytes=VMEM_LIMIT),
        name="out_mlp",
    )(x, a_out, r_out, gates, gates, ada3, ln2_w.reshape(1, D), final_w.reshape(1, D),
      wmo, wro, wout, w1, w2)


def _rotary_tables(S, inv_freq, scale):
    lane = jnp.arange(PAIR_W)
    upper = (lane % HEAD_DIM) >= HEAD_DIM // 2
    pos = jnp.arange(S, dtype=F32)
    ang = pos[:, None] * inv_freq[lane % (HEAD_DIM // 2)][None, :]
    cos, sin = jnp.cos(ang), jnp.sin(ang)
    return jnp.stack([cos, jnp.where(upper, sin, 0.0), jnp.where(upper, 0.0, -sin)]) * scale


def kernel(x, c, ln1_w, ln2_w, w_ada, b_ada, w_in, ret_gn_w, w_moba_o, w_ret_o,
           w_out, w_ff1, w_ff2, final_norm_w):
    B, S, D = x.shape
    assert w_in.shape[0] == 1, "final norm is fused into the single layer's MLP kernel"
    moba_inv = 1.0 / (ROPE_THETA ** (jnp.arange(0, HEAD_DIM, 2, dtype=F32) / HEAD_DIM))
    ret_inv = 1.0 / (ROPE_THETA ** jnp.linspace(0.0, 1.0, HEAD_DIM // 2, dtype=F32))
    qk_scale = HEAD_DIM ** -0.5
    tabs = jnp.stack([_rotary_tables(S, moba_inv, qk_scale * LOG2E),
                      _rotary_tables(S, moba_inv, 1.0),
                      _rotary_tables(S, ret_inv, 1.0),
                      _rotary_tables(S, ret_inv, qk_scale)])

    ada3 = _ada(c, w_ada[0], b_ada[0]).reshape(B, N_ADA, D)
    qkv, gates = _in_proj(x, ada3, ln1_w[0], tabs, w_in[0].astype(BF16))
    a_out = _moba(qkv)
    r_out = _ret(qkv, gates, ret_gn_w[0])
    return _out_mlp(x, a_out, r_out, gates, ada3, ln2_w[0], final_norm_w,
                    w_moba_o[0].astype(BF16), w_ret_o[0].astype(BF16), w_out[0].astype(BF16),
                    w_ff1[0].astype(BF16), w_ff2[0].astype(BF16))
```

```python
import functools
import math

import jax
import jax.numpy as jnp
from jax import lax
from jax.experimental import pallas as pl
from jax.experimental.pallas import tpu as pltpu

F32 = jnp.float32
BF16 = jnp.bfloat16

D_MODEL = 1024
N_ADA = 6
HEAD_DIM = 64
HEADS = 8
PAIR_W = 2 * HEAD_DIM
N_PAIRS = HEADS // 2
MOBA_W = HEADS * HEAD_DIM
MOBA_BLOCK = 256
MOBA_TOPK = 3
SCORE_LOOKAHEAD = 2
SUM_ROWS = 16
RET_V_DIM = 128
RET_V_W = HEADS * RET_V_DIM
RET_CHUNK = 256
D_FF = 4 * D_MODEL
FF_CHUNK = 1024
ROPE_THETA = 10000.0
EPS = 1e-6
LOG2E = math.log2(math.e)
QKV_W = 3 * MOBA_W + 2 * MOBA_W + RET_V_W
GATES_W = RET_V_W + 2 * D_MODEL
IN_WIDTH = QKV_W + GATES_W
PROJ_CHUNK = 512
TOKEN_TILE = 512
ROW_GROUP_EIGHTHS = (4, 8)
VMEM_LIMIT = 60 * 1024 * 1024

NT_DIMS = (((1,), (1,)), ((), ()))


def _sigmoid(t):
    return 1.0 / (1.0 + jnp.exp(-t))


def _rms_norm(x, w):
    return x * lax.rsqrt(jnp.mean(x * x, axis=-1, keepdims=True) + EPS) * w


def _ada_kernel(c_ref, w_ref, b_ref, o_ref):
    c = c_ref[...]
    ca = c * _sigmoid(c)
    B = c.shape[0]
    w = w_ref[...]
    c_hi = ca.astype(BF16)
    c_lo = (ca - c_hi.astype(F32)).astype(BF16)
    w_hi = w.astype(BF16)
    w_lo = (w - w_hi.astype(F32)).astype(BF16)
    main = jnp.dot(jnp.concatenate([c_hi, c_lo], axis=0), w_hi, preferred_element_type=F32)
    corr = jnp.dot(c_hi, w_lo, preferred_element_type=F32)
    o_ref[...] = main[:B] + main[B:] + corr + b_ref[...]


def _ada(c, w_ada, b_ada):
    B, D = c.shape
    N = w_ada.shape[1]
    bn = 1024
    return pl.pallas_call(
        _ada_kernel,
        grid=(N // bn,),
        in_specs=[pl.BlockSpec((B, D), lambda j: (0, 0)),
                  pl.BlockSpec((D, bn), lambda j: (0, j)),
                  pl.BlockSpec((1, bn), lambda j: (0, j))],
        out_specs=pl.BlockSpec((B, bn), lambda j: (0, j)),
        out_shape=jax.ShapeDtypeStruct((B, N), F32),
        name="ada",
    )(c, w_ada, b_ada.reshape(1, N))


def _rotary(tk, tab_ref, which, rows):
    return (tk * tab_ref[which, 0, rows, :]
            + pltpu.roll(tk, HEAD_DIM // 2, 1) * tab_ref[which, 1, rows, :]
            + pltpu.roll(tk, PAIR_W - HEAD_DIM // 2, 1) * tab_ref[which, 2, rows, :])


def _staggered(gens):
    gens, live = list(gens), []
    while gens or live:
        if gens:
            live.append(gens.pop(0))
        for gen in list(live):
            if next(gen, StopIteration) is StopIteration:
                live.remove(gen)


def _row_groups(tm):
    bounds = [0] + [tm * f // 8 for f in ROW_GROUP_EIGHTHS]
    return [slice(lo, hi) for lo, hi in zip(bounds[:-1], bounds[1:])]


def _in_proj_kernel(x_ref, ada_ref, ln_ref, tab_ref, w_ref, qkv_ref, gates_ref):
    shift = ada_ref[0, 0:1, :]
    scale = ada_ref[0, 1:2, :]
    rot = {0: 0, 1: 1, 3: 2, 4: 3}

    def stages(rows):
        h = (_rms_norm(x_ref[0, rows, :], ln_ref[...]) * (1.0 + scale) + shift).astype(BF16)
        yield
        for ci in range(IN_WIDTH // PROJ_CHUNK):
            c0 = ci * PROJ_CHUNK
            t = jnp.dot(h, w_ref[:, c0:c0 + PROJ_CHUNK], preferred_element_type=F32)
            if ci in rot:
                for i in range(PROJ_CHUNK // PAIR_W):
                    lo = i * PAIR_W
                    piece = _rotary(t[:, lo:lo + PAIR_W], tab_ref, rot[ci], rows)
                    qkv_ref[0, rows, c0 + lo:c0 + lo + PAIR_W] = piece.astype(BF16)
            elif c0 < QKV_W:
                qkv_ref[0, rows, c0:c0 + PROJ_CHUNK] = t.astype(BF16)
            else:
                gates_ref[0, rows, c0 - QKV_W:c0 - QKV_W + PROJ_CHUNK] = t.astype(BF16)
            yield

    _staggered(stages(rows) for rows in _row_groups(x_ref.shape[1]))


def _in_proj(x, ada3, ln1_w, tabs, w_in_bf16):
    B, S, D = x.shape
    tm = TOKEN_TILE
    return pl.pallas_call(
        _in_proj_kernel,
        grid=(S // tm, B),
        in_specs=[pl.BlockSpec((1, tm, D), lambda si, b: (b, si, 0)),
                  pl.BlockSpec((1, N_ADA, D), lambda si, b: (b, 0, 0)),
                  pl.BlockSpec((1, D), lambda si, b: (0, 0)),
                  pl.BlockSpec((4, 3, tm, PAIR_W), lambda si, b: (0, 0, si, 0)),
                  pl.BlockSpec((D, IN_WIDTH), lambda si, b: (0, 0),
                               pipeline_mode=pl.Buffered(1))],
        out_specs=[pl.BlockSpec((1, tm, QKV_W), lambda si, b: (b, si, 0)),
                   pl.BlockSpec((1, tm, GATES_W), lambda si, b: (b, si, 0))],
        out_shape=[jax.ShapeDtypeStruct((B, S, QKV_W), BF16),
                   jax.ShapeDtypeStruct((B, S, GATES_W), BF16)],
        compiler_params=pltpu.CompilerParams(vmem_limit_bytes=VMEM_LIMIT),
        name="in_proj",
    )(x, ada3, ln1_w.reshape(1, D), tabs, w_in_bf16)


def _moba_steps(q_ref, k_ref, v_ref, o_ref, vt_scr):
    S = q_ref.shape[1]
    T = MOBA_BLOCK
    nb = S // T
    lane = lax.broadcasted_iota(jnp.int32, (1, PAIR_W), 1)
    head_masks = [(lane >= h * HEAD_DIM) & (lane < (h + 1) * HEAD_DIM) for h in range(2)]
    krow = lax.broadcasted_iota(jnp.int32, (T, 2 * T), 0)
    qcol = lax.broadcasted_iota(jnp.int32, (T, 2 * T), 1) % T
    causal2 = krow <= qcol

    def score_steps(j, st):
        qt = q_ref[0, j * T:(j + 1) * T, :]
        zero = jnp.zeros_like(qt)
        q2 = jnp.concatenate([jnp.where(hm, qt, zero) for hm in head_masks], axis=0)
        st["blocks"], st["cmax"] = [], []
        for n in range(j + 1):
            def step(n=n):
                sn = lax.dot_general(k_ref[0, n * T:(n + 1) * T, :], q2, NT_DIMS,
                                     preferred_element_type=F32)
                if n == j:
                    sn = jnp.where(causal2, sn, -jnp.inf)
                st["blocks"].append(sn)
                st["cmax"].append(jnp.max(sn, axis=0, keepdims=True))
            yield step

    biases = []

    def prologue_steps():
        for n in range(nb):
            vt = v_ref[0, n * T:(n + 1) * T, :].T
            for h in range(2):
                vt_scr[h, n, 0:HEAD_DIM, :] = vt[h * HEAD_DIM:(h + 1) * HEAD_DIM]
                vt_scr[h, n, HEAD_DIM:, :] = jnp.ones((SUM_ROWS, T), BF16)
            if n % 2:
                yield
        kmean = jnp.mean(k_ref[0].astype(F32).reshape(nb, T, PAIR_W), axis=1)
        km_rows = []
        for hm in head_masks:
            kmh = jnp.where(hm, kmean, 0.0)
            hi = kmh.astype(BF16).astype(F32)
            mid = (kmh - hi).astype(BF16).astype(F32)
            km_rows += [hi, mid, kmh - hi - mid]
        km = jnp.concatenate(km_rows, axis=0).astype(BF16)
        gate_terms = lax.dot_general(km, q_ref[0], NT_DIMS,
                                     preferred_element_type=F32)
        yield
        blk = lax.broadcasted_iota(jnp.int32, (nb, S), 0)
        qblk = lax.broadcasted_iota(jnp.int32, (nb, S), 1) // T
        past = blk < qblk
        for h in range(2):
            gt = gate_terms[3 * nb * h:3 * nb * (h + 1)]
            gate = gt[0:nb] + gt[nb:2 * nb] + gt[2 * nb:3 * nb]
            g = jnp.where(past, gate, -jnp.inf)
            rank = jnp.zeros((nb, S), jnp.int32)
            for m in range(nb - 1):
                gm = g[m:m + 1, :]
                beats = (gm > g) | ((gm == g) & (m < blk))
                rank = rank + beats.astype(jnp.int32)
            sel = past & (rank < MOBA_TOPK)
            biases.append(jnp.where(sel, 0.0, -jnp.inf).astype(F32))
            yield

    order = list(range(nb - 1, -1, -1))
    states = [dict() for _ in range(nb)]
    first = [step for j in order[:SCORE_LOOKAHEAD] for step in score_steps(j, states[j])]
    pro = prologue_steps()
    for step in first:
        step()
        next(pro, None)
        yield
    for _ in pro:
        yield

    def value_steps(j, st):
        bias = jnp.concatenate([b[:, j * T:(j + 1) * T] for b in biases], axis=1)
        m = st["cmax"][j]
        for n in range(j):
            m = jnp.maximum(m, st["cmax"][n] + bias[n:n + 1])
        acc = [jnp.zeros((HEAD_DIM + SUM_ROWS, T), F32) for _ in range(2)]
        for n in range(j + 1):
            def step(n=n):
                off = m if n == j else m - bias[n:n + 1]
                p = jnp.exp2((st["blocks"][n] - off).astype(BF16))
                for h in range(2):
                    acc[h] = acc[h] + jnp.dot(vt_scr[h, n], p[:, h * T:(h + 1) * T],
                                              preferred_element_type=F32)
            yield step

        def finish():
            outs = [a[0:HEAD_DIM] * (1.0 / a[HEAD_DIM:HEAD_DIM + 1]) for a in acc]
            o_ref[0, j * T:(j + 1) * T, :] = jnp.concatenate(outs, axis=0).T.astype(BF16)
        yield finish

    for i, j in enumerate(order):
        ahead = i + SCORE_LOOKAHEAD
        nxt = list(score_steps(order[ahead], states[order[ahead]])) if ahead < nb else []
        cur = list(value_steps(j, states[j]))
        for t in range(max(len(nxt), len(cur))):
            if t < len(nxt):
                nxt[t]()
            if t < len(cur):
                cur[t]()
            yield
        states[j].clear()


def _ret_steps(q_ref, k_ref, v_ref, g_ref, gn_ref, dec_ref, zeta_ref, xi_ref, cdec_ref, o_ref):
    S = q_ref.shape[1]
    C = RET_CHUNK
    nc = S // C
    lane = lax.broadcasted_iota(jnp.int32, (1, PAIR_W), 1)
    state = [None, None]
    gains = [gn_ref[:, h * RET_V_DIM:(h + 1) * RET_V_DIM] * math.sqrt(RET_V_DIM) for h in range(2)]
    for n in range(nc):
        rows = slice(n * C, (n + 1) * C)
        q = q_ref[0, rows, :]
        kc = k_ref[0, rows, :]
        kt = kc.T
        for h in range(2):
            hm = (lane >= h * HEAD_DIM) & (lane < (h + 1) * HEAD_DIM)
            cols = slice(h * RET_V_DIM, (h + 1) * RET_V_DIM)
            qh = jnp.where(hm, q, jnp.zeros_like(q))
            vh = v_ref[0, rows, cols]
            s = lax.dot_general(qh, kc, NT_DIMS, preferred_element_type=F32)
            p = s.astype(BF16) * dec_ref[h]
            o = jnp.dot(p, vh, preferred_element_type=F32)
            if n > 0:
                o = o + jnp.dot(qh, state[h].astype(BF16), preferred_element_type=F32) * xi_ref[h]
            if n < nc - 1:
                kz = kt * zeta_ref[h]
                kv = jnp.dot(kz, vh, preferred_element_type=F32)
                state[h] = kv if n == 0 else state[h] * cdec_ref[h] + kv
            yield
            d = o - jnp.sum(o, axis=-1, keepdims=True) * (1.0 / RET_V_DIM)
            r = lax.rsqrt(jnp.sum(d * d, axis=-1, keepdims=True) + RET_V_DIM * EPS)
            y = d * r * gains[h]
            hg = 0.5 * g_ref[0, rows, cols].astype(F32)
            o_ref[0, rows, cols] = (hg * (1.0 + jnp.tanh(hg)) * y).astype(BF16)
            yield


def _moba_kernel(*refs):
    for _ in _moba_steps(*refs):
        pass


def _ret_kernel(*refs):
    for _ in _ret_steps(*refs):
        pass


def _ret_tables():
    C = RET_CHUNK
    log_g = jnp.log(1.0 - jnp.power(2.0, -5.0 - jnp.arange(HEADS, dtype=F32)))
    i = jnp.arange(C, dtype=F32)
    diff = i[:, None] - i[None, :]
    dec = jnp.where(diff >= 0, jnp.exp(jnp.maximum(diff, 0.0)[None] * log_g[:, None, None]), 0.0)
    zeta = jnp.exp((C - 1 - i)[None, :] * log_g[:, None])[:, None, :]
    xi = jnp.exp((i + 1)[None, :] * log_g[:, None])
    xi = jnp.broadcast_to(xi[:, :, None], (HEADS, C, RET_V_DIM))
    cdec = jnp.broadcast_to(jnp.exp(C * log_g)[:, None, None], (HEADS, 1, RET_V_DIM))
    return dec.astype(BF16), zeta.astype(BF16), xi, cdec


def _pair_block(col_off):
    return lambda b, p: (b, 0, col_off // PAIR_W + p)


def _moba(qkv):
    B, S, _ = qkv.shape
    nb = S // MOBA_BLOCK
    return pl.pallas_call(
        _moba_kernel,
        grid=(B, N_PAIRS),
        in_specs=[pl.BlockSpec((1, S, PAIR_W), _pair_block(0)),
                  pl.BlockSpec((1, S, PAIR_W), _pair_block(MOBA_W)),
                  pl.BlockSpec((1, S, PAIR_W), _pair_block(2 * MOBA_W))],
        out_specs=pl.BlockSpec((1, S, PAIR_W), _pair_block(0)),
        out_shape=jax.ShapeDtypeStruct((B, S, MOBA_W), BF16),
        scratch_shapes=[pltpu.VMEM((2, nb, HEAD_DIM + SUM_ROWS, MOBA_BLOCK), BF16)],
        name="moba",
    )(qkv, qkv, qkv)


def _ret(qkv, gates, gn_w):
    B, S, _ = qkv.shape
    C = RET_CHUNK
    v_blk = 5 * MOBA_W // (2 * RET_V_DIM)
    dec, zeta, xi, cdec = _ret_tables()
    return pl.pallas_call(
        _ret_kernel,
        grid=(B, N_PAIRS),
        in_specs=[pl.BlockSpec((1, S, PAIR_W), _pair_block(3 * MOBA_W)),
                  pl.BlockSpec((1, S, PAIR_W), _pair_block(4 * MOBA_W)),
                  pl.BlockSpec((1, S, 2 * RET_V_DIM), lambda b, p: (b, 0, v_blk + p)),
                  pl.BlockSpec((1, S, 2 * RET_V_DIM), lambda b, p: (b, 0, p)),
                  pl.BlockSpec((1, 2 * RET_V_DIM), lambda b, p: (0, p)),
                  pl.BlockSpec((2, C, C), lambda b, p: (p, 0, 0)),
                  pl.BlockSpec((2, 1, C), lambda b, p: (p, 0, 0)),
                  pl.BlockSpec((2, C, RET_V_DIM), lambda b, p: (p, 0, 0)),
                  pl.BlockSpec((2, 1, RET_V_DIM), lambda b, p: (p, 0, 0))],
        out_specs=pl.BlockSpec((1, S, 2 * RET_V_DIM), lambda b, p: (b, 0, p)),
        out_shape=jax.ShapeDtypeStruct((B, S, RET_V_W), BF16),
        name="ret",
    )(qkv, qkv, qkv, gates, gn_w.reshape(1, RET_V_W), dec, zeta, xi, cdec)


def _out_mlp_kernel(x_ref, a_ref, r_ref, ga_ref, gr_ref, ada_ref, ln2_ref, fw_ref,
                    wmo_ref, wro_ref, wout_ref, w1_ref, w2_ref, o_ref):
    gate1 = ada_ref[0, 2:3, :]
    shift2 = ada_ref[0, 3:4, :]
    scale2 = ada_ref[0, 4:5, :]
    gate2 = ada_ref[0, 5:6, :]

    def stages(rows):
        ya = jnp.dot(a_ref[0, rows, :], wmo_ref[...], preferred_element_type=F32)
        yr = jnp.dot(r_ref[0, rows, :], wro_ref[...], preferred_element_type=F32)
        yield
        merged = (_sigmoid(ga_ref[0, rows, :].astype(F32)) * ya
                  + _sigmoid(gr_ref[0, rows, :].astype(F32)) * yr).astype(BF16)
        yield
        mo = jnp.dot(merged, wout_ref[...], preferred_element_type=F32)
        yield
        x1 = x_ref[0, rows, :] + gate1 * mo
        h2 = (_rms_norm(x1, ln2_ref[...]) * (1.0 + scale2) + shift2).astype(BF16)
        yield
        acc = jnp.zeros_like(x1)
        for ci in range(D_FF // FF_CHUNK):
            c0 = ci * FF_CHUNK
            hid = jnp.dot(h2, w1_ref[:, c0:c0 + FF_CHUNK], preferred_element_type=F32)
            yield
            hid = jnp.square(jnp.maximum(hid, 0.0)).astype(BF16)
            yield
            acc = acc + jnp.dot(hid, w2_ref[c0:c0 + FF_CHUNK, :], preferred_element_type=F32)
            yield
        x2 = x1 + gate2 * acc
        o_ref[0, rows, :] = _rms_norm(x2, fw_ref[...])

    _staggered(stages(rows) for rows in _row_groups(x_ref.shape[1]))


def _out_mlp(x, a_out, r_out, gates, ada3, ln2_w, final_w, wmo, wro, wout, w1, w2):
    B, S, D = x.shape
    tm = TOKEN_TILE
    const = functools.partial(pl.BlockSpec, pipeline_mode=pl.Buffered(1))
    return pl.pallas_call(
        _out_mlp_kernel,
        grid=(B, S // tm),
        in_specs=[pl.BlockSpec((1, tm, D), lambda b, si: (b, si, 0)),
                  pl.BlockSpec((1, tm, MOBA_W), lambda b, si: (b, si, 0)),
                  pl.BlockSpec((1, tm, RET_V_W), lambda b, si: (b, si, 0)),
                  pl.BlockSpec((1, tm, D), lambda b, si: (b, si, 1)),
                  pl.BlockSpec((1, tm, D), lambda b, si: (b, si, 2)),
                  pl.BlockSpec((1, N_ADA, D), lambda b, si: (b, 0, 0)),
                  pl.BlockSpec((1, D), lambda b, si: (0, 0)),
                  pl.BlockSpec((1, D), lambda b, si: (0, 0)),
                  const((MOBA_W, D), lambda b, si: (0, 0)),
                  const((RET_V_W, D), lambda b, si: (0, 0)),
                  const((D, D), lambda b, si: (0, 0)),
                  const((D, D_FF), lambda b, si: (0, 0)),
                  const((D_FF, D), lambda b, si: (0, 0))],
        out_specs=pl.BlockSpec((1, tm, D), lambda b, si: (b, si, 0)),
        out_shape=jax.ShapeDtypeStruct((B, S, D), F32),
        compiler_params=pltpu.CompilerParams(vmem_limit_bytes=VMEM_LIMIT),
        name="out_mlp",
    )(x, a_out, r_out, gates, gates, ada3, ln2_w.reshape(1, D), final_w.reshape(1, D),
      wmo, wro, wout, w1, w2)


def _rotary_tables(S, inv_freq, scale):
    lane = jnp.arange(PAIR_W)
    upper = (lane % HEAD_DIM) >= HEAD_DIM // 2
    pos = jnp.arange(S, dtype=F32)
    ang = pos[:, None] * inv_freq[lane % (HEAD_DIM // 2)][None, :]
    cos, sin = jnp.cos(ang), jnp.sin(ang)
    return jnp.stack([cos, jnp.where(upper, sin, 0.0), jnp.where(upper, 0.0, -sin)]) * scale


def kernel(x, c, ln1_w, ln2_w, w_ada, b_ada, w_in, ret_gn_w, w_moba_o, w_ret_o,
           w_out, w_ff1, w_ff2, final_norm_w):
    B, S, D = x.shape
    assert w_in.shape[0] == 1, "final norm is fused into the single layer's MLP kernel"
    moba_inv = 1.0 / (ROPE_THETA ** (jnp.arange(0, HEAD_DIM, 2, dtype=F32) / HEAD_DIM))
    ret_inv = 1.0 / (ROPE_THETA ** jnp.linspace(0.0, 1.0, HEAD_DIM // 2, dtype=F32))
    qk_scale = HEAD_DIM ** -0.5
    tabs = jnp.stack([_rotary_tables(S, moba_inv, qk_scale * LOG2E),
                      _rotary_tables(S, moba_inv, 1.0),
                      _rotary_tables(S, ret_inv, 1.0),
                      _rotary_tables(S, ret_inv, qk_scale)])

    ada3 = _ada(c, w_ada[0], b_ada[0]).reshape(B, N_ADA, D)
    qkv, gates = _in_proj(x, ada3, ln1_w[0], tabs, w_in[0].astype(BF16))
    a_out = _moba(qkv)
    r_out = _ret(qkv, gates, ret_gn_w[0])
    return _out_mlp(x, a_out, r_out, gates, ada3, ln2_w[0], final_norm_w,
                    w_moba_o[0].astype(BF16), w_ret_o[0].astype(BF16), w_out[0].astype(BF16),
                    w_ff1[0].astype(BF16), w_ff2[0].astype(BF16))
```

```python
import functools
import math

import jax
import jax.numpy as jnp
from jax import lax
from jax.experimental import pallas as pl
from jax.experimental.pallas import tpu as pltpu

F32 = jnp.float32
BF16 = jnp.bfloat16

D_MODEL = 1024
N_ADA = 6
HEAD_DIM = 64
HEADS = 8
PAIR_W = 2 * HEAD_DIM
N_PAIRS = HEADS // 2
MOBA_W = HEADS * HEAD_DIM
MOBA_BLOCK = 256
MOBA_TOPK = 3
SCORE_LOOKAHEAD = 2
SUM_ROWS = 16
RET_V_DIM = 128
RET_V_W = HEADS * RET_V_DIM
RET_CHUNK = 256
D_FF = 4 * D_MODEL
FF_CHUNK = 1024
ROPE_THETA = 10000.0
EPS = 1e-6
LOG2E = math.log2(math.e)
QKV_W = 3 * MOBA_W + 2 * MOBA_W + RET_V_W
GATES_W = RET_V_W + 2 * D_MODEL
IN_WIDTH = QKV_W + GATES_W
PROJ_CHUNK = 512
TOKEN_TILE = 512
ROW_GROUP_EIGHTHS = (4, 8)
VMEM_LIMIT = 60 * 1024 * 1024

NT_DIMS = (((1,), (1,)), ((), ()))


def _sigmoid(t):
    return 1.0 / (1.0 + jnp.exp(-t))


def _rms_norm(x, w):
    return x * lax.rsqrt(jnp.mean(x * x, axis=-1, keepdims=True) + EPS) * w


def _ada_kernel(c_ref, w_ref, b_ref, o_ref):
    c = c_ref[...]
    ca = c * _sigmoid(c)
    B = c.shape[0]
    w = w_ref[...]
    c_hi = ca.astype(BF16)
    c_lo = (ca - c_hi.astype(F32)).astype(BF16)
    w_hi = w.astype(BF16)
    w_lo = (w - w_hi.astype(F32)).astype(BF16)
    main = jnp.dot(jnp.concatenate([c_hi, c_lo], axis=0), w_hi, preferred_element_type=F32)
    corr = jnp.dot(c_hi, w_lo, preferred_element_type=F32)
    o_ref[...] = main[:B] + main[B:] + corr + b_ref[...]


def _ada(c, w_ada, b_ada):
    B, D = c.shape
    N = w_ada.shape[1]
    bn = 1024
    return pl.pallas_call(
        _ada_kernel,
        grid=(N // bn,),
        in_specs=[pl.BlockSpec((B, D), lambda j: (0, 0)),
                  pl.BlockSpec((D, bn), lambda j: (0, j)),
                  pl.BlockSpec((1, bn), lambda j: (0, j))],
        out_specs=pl.BlockSpec((B, bn), lambda j: (0, j)),
        out_shape=jax.ShapeDtypeStruct((B, N), F32),
        name="ada",
    )(c, w_ada, b_ada.reshape(1, N))


def _rotary(tk, tab_ref, which, rows):
    return (tk * tab_ref[which, 0, rows, :]
            + pltpu.roll(tk, HEAD_DIM // 2, 1) * tab_ref[which, 1, rows, :]
            + pltpu.roll(tk, PAIR_W - HEAD_DIM // 2, 1) * tab_ref[which, 2, rows, :])


def _staggered(gens):
    gens, live = list(gens), []
    while gens or live:
        if gens:
            live.append(gens.pop(0))
        for gen in list(live):
            if next(gen, StopIteration) is StopIteration:
                live.remove(gen)


def _row_groups(tm):
    bounds = [0] + [tm * f // 8 for f in ROW_GROUP_EIGHTHS]
    return [slice(lo, hi) for lo, hi in zip(bounds[:-1], bounds[1:])]


def _in_proj_kernel(x_ref, ada_ref, ln_ref, tab_ref, w_ref, qkv_ref, gates_ref):
    shift = ada_ref[0, 0:1, :]
    scale = ada_ref[0, 1:2, :]
    rot = {0: 0, 1: 1, 3: 2, 4: 3}

    def stages(rows):
        h = (_rms_norm(x_ref[0, rows, :], ln_ref[...]) * (1.0 + scale) + shift).astype(BF16)
        yield
        for ci in range(IN_WIDTH // PROJ_CHUNK):
            c0 = ci * PROJ_CHUNK
            t = jnp.dot(h, w_ref[:, c0:c0 + PROJ_CHUNK], preferred_element_type=F32)
            if ci in rot:
                for i in range(PROJ_CHUNK // PAIR_W):
                    lo = i * PAIR_W
                    piece = _rotary(t[:, lo:lo + PAIR_W], tab_ref, rot[ci], rows)
                    qkv_ref[0, rows, c0 + lo:c0 + lo + PAIR_W] = piece.astype(BF16)
            elif c0 < QKV_W:
                qkv_ref[0, rows, c0:c0 + PROJ_CHUNK] = t.astype(BF16)
            else:
                gates_ref[0, rows, c0 - QKV_W:c0 - QKV_W + PROJ_CHUNK] = t.astype(BF16)
            yield

    _staggered(stages(rows) for rows in _row_groups(x_ref.shape[1]))


def _in_proj(x, ada3, ln1_w, tabs, w_in_bf16):
    B, S, D = x.shape
    tm = TOKEN_TILE
    return pl.pallas_call(
        _in_proj_kernel,
        grid=(S // tm, B),
        in_specs=[pl.BlockSpec((1, tm, D), lambda si, b: (b, si, 0)),
                  pl.BlockSpec((1, N_ADA, D), lambda si, b: (b, 0, 0)),
                  pl.BlockSpec((1, D), lambda si, b: (0, 0)),
                  pl.BlockSpec((4, 3, tm, PAIR_W), lambda si, b: (0, 0, si, 0)),
                  pl.BlockSpec((D, IN_WIDTH), lambda si, b: (0, 0),
                               pipeline_mode=pl.Buffered(1))],
        out_specs=[pl.BlockSpec((1, tm, QKV_W), lambda si, b: (b, si, 0)),
                   pl.BlockSpec((1, tm, GATES_W), lambda si, b: (b, si, 0))],
        out_shape=[jax.ShapeDtypeStruct((B, S, QKV_W), BF16),
                   jax.ShapeDtypeStruct((B, S, GATES_W), BF16)],
        compiler_params=pltpu.CompilerParams(vmem_limit_bytes=VMEM_LIMIT),
        name="in_proj",
    )(x, ada3, ln1_w.reshape(1, D), tabs, w_in_bf16)


def _moba_steps(q_ref, k_ref, v_ref, o_ref, vt_scr):
    S = q_ref.shape[1]
    T = MOBA_BLOCK
    nb = S // T
    lane = lax.broadcasted_iota(jnp.int32, (1, PAIR_W), 1)
    head_masks = [(lane >= h * HEAD_DIM) & (lane < (h + 1) * HEAD_DIM) for h in range(2)]
    krow = lax.broadcasted_iota(jnp.int32, (T, 2 * T), 0)
    qcol = lax.broadcasted_iota(jnp.int32, (T, 2 * T), 1) % T
    causal2 = krow <= qcol

    def score_steps(j, st):
        qt = q_ref[0, j * T:(j + 1) * T, :]
        zero = jnp.zeros_like(qt)
        q2 = jnp.concatenate([jnp.where(hm, qt, zero) for hm in head_masks], axis=0)
        st["blocks"], st["cmax"] = [], []
        for n in range(j + 1):
            def step(n=n):
                sn = lax.dot_general(k_ref[0, n * T:(n + 1) * T, :], q2, NT_DIMS,
                                     preferred_element_type=F32)
                if n == j:
                    sn = jnp.where(causal2, sn, -jnp.inf)
                st["blocks"].append(sn)
                st["cmax"].append(jnp.max(sn, axis=0, keepdims=True))
            yield step

    biases = []

    def prologue_steps():
        for n in range(nb):
            vt = v_ref[0, n * T:(n + 1) * T, :].T
            for h in range(2):
                vt_scr[h, n, 0:HEAD_DIM, :] = vt[h * HEAD_DIM:(h + 1) * HEAD_DIM]
                vt_scr[h, n, HEAD_DIM:, :] = jnp.ones((SUM_ROWS, T), BF16)
            if n % 2:
                yield
        kmean = jnp.mean(k_ref[0].astype(F32).reshape(nb, T, PAIR_W), axis=1)
        km_rows = []
        for hm in head_masks:
            kmh = jnp.where(hm, kmean, 0.0)
            hi = kmh.astype(BF16).astype(F32)
            mid = (kmh - hi).astype(BF16).astype(F32)
            km_rows += [hi, mid, kmh - hi - mid]
        km = jnp.concatenate(km_rows, axis=0).astype(BF16)
        gate_terms = lax.dot_general(km, q_ref[0], NT_DIMS,
                                     preferred_element_type=F32)
        yield
        blk = lax.broadcasted_iota(jnp.int32, (nb, S), 0)
        qblk = lax.broadcasted_iota(jnp.int32, (nb, S), 1) // T
        past = blk < qblk
        for h in range(2):
            gt = gate_terms[3 * nb * h:3 * nb * (h + 1)]
            gate = gt[0:nb] + gt[nb:2 * nb] + gt[2 * nb:3 * nb]
            g = jnp.where(past, gate, -jnp.inf)
            rank = jnp.zeros((nb, S), jnp.int32)
            for m in range(nb - 1):
                gm = g[m:m + 1, :]
                beats = (gm > g) | ((gm == g) & (m < blk))
                rank = rank + beats.astype(jnp.int32)
            sel = past & (rank < MOBA_TOPK)
            biases.append(jnp.where(sel, 0.0, -jnp.inf).astype(F32))
            yield

    order = list(range(nb - 1, -1, -1))
    states = [dict() for _ in range(nb)]
    first = [step for j in order[:SCORE_LOOKAHEAD] for step in score_steps(j, states[j])]
    pro = prologue_steps()
    for step in first:
        step()
        next(pro, None)
        yield
    for _ in pro:
        yield

    def value_steps(j, st):
        bias = jnp.concatenate([b[:, j * T:(j + 1) * T] for b in biases], axis=1)
        m = st["cmax"][j]
        for n in range(j):
            m = jnp.maximum(m, st["cmax"][n] + bias[n:n + 1])
        acc = [jnp.zeros((HEAD_DIM + SUM_ROWS, T), F32) for _ in range(2)]
        for n in range(j + 1):
            def step(n=n):
                off = m if n == j else m - bias[n:n + 1]
                p = jnp.exp2(st["blocks"][n] - off).astype(BF16)
                for h in range(2):
                    acc[h] = acc[h] + jnp.dot(vt_scr[h, n], p[:, h * T:(h + 1) * T],
                                              preferred_element_type=F32)
            yield step

        def finish():
            outs = [a[0:HEAD_DIM] * (1.0 / a[HEAD_DIM:HEAD_DIM + 1]) for a in acc]
            o_ref[0, j * T:(j + 1) * T, :] = jnp.concatenate(outs, axis=0).T.astype(BF16)
        yield finish

    for i, j in enumerate(order):
        ahead = i + SCORE_LOOKAHEAD
        nxt = list(score_steps(order[ahead], states[order[ahead]])) if ahead < nb else []
        cur = list(value_steps(j, states[j]))
        for t in range(max(len(nxt), len(cur))):
            if t < len(nxt):
                nxt[t]()
            if t < len(cur):
                cur[t]()
            yield
        states[j].clear()


def _ret_steps(q_ref, k_ref, v_ref, g_ref, gn_ref, dec_ref, zeta_ref, xi_ref, cdec_ref, o_ref):
    S = q_ref.shape[1]
    C = RET_CHUNK
    nc = S // C
    lane = lax.broadcasted_iota(jnp.int32, (1, PAIR_W), 1)
    state = [None, None]
    gains = [gn_ref[:, h * RET_V_DIM:(h + 1) * RET_V_DIM] * math.sqrt(RET_V_DIM) for h in range(2)]
    for n in range(nc):
        rows = slice(n * C, (n + 1) * C)
        q = q_ref[0, rows, :]
        kc = k_ref[0, rows, :]
        kt = kc.T
        for h in range(2):
            hm = (lane >= h * HEAD_DIM) & (lane < (h + 1) * HEAD_DIM)
            cols = slice(h * RET_V_DIM, (h + 1) * RET_V_DIM)
            qh = jnp.where(hm, q, jnp.zeros_like(q))
            vh = v_ref[0, rows, cols]
            s = lax.dot_general(qh, kc, NT_DIMS, preferred_element_type=F32)
            p = s.astype(BF16) * dec_ref[h]
            o = jnp.dot(p, vh, preferred_element_type=F32)
            if n > 0:
                o = o + jnp.dot(qh, state[h].astype(BF16), preferred_element_type=F32) * xi_ref[h]
            if n < nc - 1:
                kz = kt * zeta_ref[h]
                kv = jnp.dot(kz, vh, preferred_element_type=F32)
                state[h] = kv if n == 0 else state[h] * cdec_ref[h] + kv
            yield
            d = o - jnp.sum(o, axis=-1, keepdims=True) * (1.0 / RET_V_DIM)
            r = lax.rsqrt(jnp.sum(d * d, axis=-1, keepdims=True) + RET_V_DIM * EPS)
            y = d * r * gains[h]
            hg = 0.5 * g_ref[0, rows, cols].astype(F32)
            o_ref[0, rows, cols] = (hg * (1.0 + jnp.tanh(hg)) * y).astype(BF16)
            yield


def _moba_kernel(*refs):
    for _ in _moba_steps(*refs):
        pass


def _ret_kernel(*refs):
    for _ in _ret_steps(*refs):
        pass


def _ret_tables():
    C = RET_CHUNK
    log_g = jnp.log(1.0 - jnp.power(2.0, -5.0 - jnp.arange(HEADS, dtype=F32)))
    i = jnp.arange(C, dtype=F32)
    diff = i[:, None] - i[None, :]
    dec = jnp.where(diff >= 0, jnp.exp(jnp.maximum(diff, 0.0)[None] * log_g[:, None, None]), 0.0)
    zeta = jnp.exp((C - 1 - i)[None, :] * log_g[:, None])[:, None, :]
    xi = jnp.exp((i + 1)[None, :] * log_g[:, None])
    xi = jnp.broadcast_to(xi[:, :, None], (HEADS, C, RET_V_DIM))
    cdec = jnp.broadcast_to(jnp.exp(C * log_g)[:, None, None], (HEADS, 1, RET_V_DIM))
    return dec.astype(BF16), zeta.astype(BF16), xi, cdec


def _pair_block(col_off):
    return lambda b, p: (b, 0, col_off // PAIR_W + p)


def _moba(qkv):
    B, S, _ = qkv.shape
    nb = S // MOBA_BLOCK
    return pl.pallas_call(
        _moba_kernel,
        grid=(B, N_PAIRS),
        in_specs=[pl.BlockSpec((1, S, PAIR_W), _pair_block(0)),
                  pl.BlockSpec((1, S, PAIR_W), _pair_block(MOBA_W)),
                  pl.BlockSpec((1, S, PAIR_W), _pair_block(2 * MOBA_W))],
        out_specs=pl.BlockSpec((1, S, PAIR_W), _pair_block(0)),
        out_shape=jax.ShapeDtypeStruct((B, S, MOBA_W), BF16),
        scratch_shapes=[pltpu.VMEM((2, nb, HEAD_DIM + SUM_ROWS, MOBA_BLOCK), BF16)],
        name="moba",
    )(qkv, qkv, qkv)


def _ret(qkv, gates, gn_w):
    B, S, _ = qkv.shape
    C = RET_CHUNK
    v_blk = 5 * MOBA_W // (2 * RET_V_DIM)
    dec, zeta, xi, cdec = _ret_tables()
    return pl.pallas_call(
        _ret_kernel,
        grid=(B, N_PAIRS),
        in_specs=[pl.BlockSpec((1, S, PAIR_W), _pair_block(3 * MOBA_W)),
                  pl.BlockSpec((1, S, PAIR_W), _pair_block(4 * MOBA_W)),
                  pl.BlockSpec((1, S, 2 * RET_V_DIM), lambda b, p: (b, 0, v_blk + p)),
                  pl.BlockSpec((1, S, 2 * RET_V_DIM), lambda b, p: (b, 0, p)),
                  pl.BlockSpec((1, 2 * RET_V_DIM), lambda b, p: (0, p)),
                  pl.BlockSpec((2, C, C), lambda b, p: (p, 0, 0)),
                  pl.BlockSpec((2, 1, C), lambda b, p: (p, 0, 0)),
                  pl.BlockSpec((2, C, RET_V_DIM), lambda b, p: (p, 0, 0)),
                  pl.BlockSpec((2, 1, RET_V_DIM), lambda b, p: (p, 0, 0))],
        out_specs=pl.BlockSpec((1, S, 2 * RET_V_DIM), lambda b, p: (b, 0, p)),
        out_shape=jax.ShapeDtypeStruct((B, S, RET_V_W), BF16),
        name="ret",
    )(qkv, qkv, qkv, gates, gn_w.reshape(1, RET_V_W), dec, zeta, xi, cdec)


def _out_mlp_kernel(x_ref, a_ref, r_ref, ga_ref, gr_ref, ada_ref, ln2_ref, fw_ref,
                    wmo_ref, wro_ref, wout_ref, w1_ref, w2_ref, o_ref):
    gate1 = ada_ref[0, 2:3, :]
    shift2 = ada_ref[0, 3:4, :]
    scale2 = ada_ref[0, 4:5, :]
    gate2 = ada_ref[0, 5:6, :]

    def stages(rows):
        ya = jnp.dot(a_ref[0, rows, :], wmo_ref[...], preferred_element_type=F32)
        yr = jnp.dot(r_ref[0, rows, :], wro_ref[...], preferred_element_type=F32)
        yield
        merged = (_sigmoid(ga_ref[0, rows, :].astype(F32)) * ya
                  + _sigmoid(gr_ref[0, rows, :].astype(F32)) * yr).astype(BF16)
        yield
        mo = jnp.dot(merged, wout_ref[...], preferred_element_type=F32)
        yield
        x1 = x_ref[0, rows, :] + gate1 * mo
        h2 = (_rms_norm(x1, ln2_ref[...]) * (1.0 + scale2) + shift2).astype(BF16)
        yield
        acc = jnp.zeros_like(x1)
        for ci in range(D_FF // FF_CHUNK):
            c0 = ci * FF_CHUNK
            hid = jnp.dot(h2, w1_ref[:, c0:c0 + FF_CHUNK], preferred_element_type=F32)
            yield
            hid = jnp.square(jnp.maximum(hid, 0.0)).astype(BF16)
            yield
            acc = acc + jnp.dot(hid, w2_ref[c0:c0 + FF_CHUNK, :], preferred_element_type=F32)
            yield
        x2 = x1 + gate2 * acc
        o_ref[0, rows, :] = _rms_norm(x2, fw_ref[...])

    _staggered(stages(rows) for rows in _row_groups(x_ref.shape[1]))


def _out_mlp(x, a_out, r_out, gates, ada3, ln2_w, final_w, wmo, wro, wout, w1, w2):
    B, S, D = x.shape
    tm = TOKEN_TILE
    const = functools.partial(pl.BlockSpec, pipeline_mode=pl.Buffered(1))
    return pl.pallas_call(
        _out_mlp_kernel,
        grid=(B, S // tm),
        in_specs=[pl.BlockSpec((1, tm, D), lambda b, si: (b, si, 0)),
                  pl.BlockSpec((1, tm, MOBA_W), lambda b, si: (b, si, 0)),
                  pl.BlockSpec((1, tm, RET_V_W), lambda b, si: (b, si, 0)),
                  pl.BlockSpec((1, tm, D), lambda b, si: (b, si, 1)),
                  pl.BlockSpec((1, tm, D), lambda b, si: (b, si, 2)),
                  pl.BlockSpec((1, N_ADA, D), lambda b, si: (b, 0, 0)),
                  pl.BlockSpec((1, D), lambda b, si: (0, 0)),
                  pl.BlockSpec((1, D), lambda b, si: (0, 0)),
                  const((MOBA_W, D), lambda b, si: (0, 0)),
                  const((RET_V_W, D), lambda b, si: (0, 0)),
                  const((D, D), lambda b, si: (0, 0)),
                  const((D, D_FF), lambda b, si: (0, 0)),
                  const((D_FF, D), lambda b, si: (0, 0))],
        out_specs=pl.BlockSpec((1, tm, D), lambda b, si: (b, si, 0)),
        out_shape=jax.ShapeDtypeStruct((B, S, D), F32),
        compiler_params=pltpu.CompilerParams(vmem_limit_bytes=VMEM_LIMIT),
        name="out_mlp",
    )(x, a_out, r_out, gates, gates, ada3, ln2_w.reshape(1, D), final_w.reshape(1, D),
      wmo, wro, wout, w1, w2)


def _rotary_tables(S, inv_freq, scale):
    lane = jnp.arange(PAIR_W)
    upper = (lane % HEAD_DIM) >= HEAD_DIM // 2
    pos = jnp.arange(S, dtype=F32)
    ang = pos[:, None] * inv_freq[lane % (HEAD_DIM // 2)][None, :]
    cos, sin = jnp.cos(ang), jnp.sin(ang)
    return jnp.stack([cos, jnp.where(upper, sin, 0.0), jnp.where(upper, 0.0, -sin)]) * scale


def kernel(x, c, ln1_w, ln2_w, w_ada, b_ada, w_in, ret_gn_w, w_moba_o, w_ret_o,
           w_out, w_ff1, w_ff2, final_norm_w):
    B, S, D = x.shape
    assert w_in.shape[0] == 1, "final norm is fused into the single layer's MLP kernel"
    moba_inv = 1.0 / (ROPE_THETA ** (jnp.arange(0, HEAD_DIM, 2, dtype=F32) / HEAD_DIM))
    ret_inv = 1.0 / (ROPE_THETA ** jnp.linspace(0.0, 1.0, HEAD_DIM // 2, dtype=F32))
    qk_scale = HEAD_DIM ** -0.5
    tabs = jnp.stack([_rotary_tables(S, moba_inv, qk_scale * LOG2E),
                      _rotary_tables(S, moba_inv, 1.0),
                      _rotary_tables(S, ret_inv, 1.0),
                      _rotary_tables(S, ret_inv, qk_scale)])

    ada3 = _ada(c, w_ada[0], b_ada[0]).reshape(B, N_ADA, D)
    qkv, gates = _in_proj(x, ada3, ln1_w[0], tabs, w_in[0].astype(BF16))
    a_out = _moba(qkv)
    r_out = _ret(qkv, gates, ret_gn_w[0])
    return _out_mlp(x, a_out, r_out, gates, ada3, ln2_w[0], final_norm_w,
                    w_moba_o[0].astype(BF16), w_ret_o[0].astype(BF16), w_out[0].astype(BF16),
                    w_ff1[0].astype(BF16), w_ff2[0].astype(BF16))
```

```python
import functools
import math

import jax
import jax.numpy as jnp
from jax import lax
from jax.experimental import pallas as pl
from jax.experimental.pallas import tpu as pltpu

F32 = jnp.float32
BF16 = jnp.bfloat16

D_MODEL = 1024
N_ADA = 6
HEAD_DIM = 64
HEADS = 8
PAIR_W = 2 * HEAD_DIM
N_PAIRS = HEADS // 2
MOBA_W = HEADS * HEAD_DIM
MOBA_BLOCK = 256
MOBA_TOPK = 3
SCORE_LOOKAHEAD = 2
SUM_ROWS = 16
RET_V_DIM = 128
RET_V_W = HEADS * RET_V_DIM
RET_CHUNK = 256
D_FF = 4 * D_MODEL
FF_CHUNK = 1024
ROPE_THETA = 10000.0
EPS = 1e-6
LOG2E = math.log2(math.e)
QKV_W = 3 * MOBA_W + 2 * MOBA_W + RET_V_W
GATES_W = RET_V_W + 2 * D_MODEL
IN_WIDTH = QKV_W + GATES_W
PROJ_CHUNK = 512
TOKEN_TILE = 512
ROW_GROUP_EIGHTHS = (4, 8)
VMEM_LIMIT = 60 * 1024 * 1024

NT_DIMS = (((1,), (1,)), ((), ()))


def _sigmoid(t):
    return 0.5 * jnp.tanh(0.5 * t) + 0.5


def _rms_norm(x, w):
    return x * lax.rsqrt(jnp.mean(x * x, axis=-1, keepdims=True) + EPS) * w


def _ada_kernel(c_ref, w_ref, b_ref, o_ref):
    c = c_ref[...]
    ca = c * _sigmoid(c)
    B = c.shape[0]
    w = w_ref[...]
    c_hi = ca.astype(BF16)
    c_lo = (ca - c_hi.astype(F32)).astype(BF16)
    w_hi = w.astype(BF16)
    w_lo = (w - w_hi.astype(F32)).astype(BF16)
    main = jnp.dot(jnp.concatenate([c_hi, c_lo], axis=0), w_hi, preferred_element_type=F32)
    corr = jnp.dot(c_hi, w_lo, preferred_element_type=F32)
    o_ref[...] = main[:B] + main[B:] + corr + b_ref[...]


def _ada(c, w_ada, b_ada):
    B, D = c.shape
    N = w_ada.shape[1]
    bn = 1024
    return pl.pallas_call(
        _ada_kernel,
        grid=(N // bn,),
        in_specs=[pl.BlockSpec((B, D), lambda j: (0, 0)),
                  pl.BlockSpec((D, bn), lambda j: (0, j)),
                  pl.BlockSpec((1, bn), lambda j: (0, j))],
        out_specs=pl.BlockSpec((B, bn), lambda j: (0, j)),
        out_shape=jax.ShapeDtypeStruct((B, N), F32),
        name="ada",
    )(c, w_ada, b_ada.reshape(1, N))


def _rotary(tk, tab_ref, which, rows):
    return (tk * tab_ref[which, 0, rows, :]
            + pltpu.roll(tk, HEAD_DIM // 2, 1) * tab_ref[which, 1, rows, :]
            + pltpu.roll(tk, PAIR_W - HEAD_DIM // 2, 1) * tab_ref[which, 2, rows, :])


def _staggered(gens):
    gens, live = list(gens), []
    while gens or live:
        if gens:
            live.append(gens.pop(0))
        for gen in list(live):
            if next(gen, StopIteration) is StopIteration:
                live.remove(gen)


def _row_groups(tm):
    bounds = [0] + [tm * f // 8 for f in ROW_GROUP_EIGHTHS]
    return [slice(lo, hi) for lo, hi in zip(bounds[:-1], bounds[1:])]


def _in_proj_kernel(x_ref, ada_ref, ln_ref, tab_ref, w_ref, qkv_ref, gates_ref):
    shift = ada_ref[0, 0:1, :]
    scale = ada_ref[0, 1:2, :]
    rot = {0: 0, 1: 1, 3: 2, 4: 3}

    def stages(rows):
        h = (_rms_norm(x_ref[0, rows, :], ln_ref[...]) * (1.0 + scale) + shift).astype(BF16)
        yield
        for ci in range(IN_WIDTH // PROJ_CHUNK):
            c0 = ci * PROJ_CHUNK
            t = jnp.dot(h, w_ref[:, c0:c0 + PROJ_CHUNK], preferred_element_type=F32)
            if ci in rot:
                for i in range(PROJ_CHUNK // PAIR_W):
                    lo = i * PAIR_W
                    piece = _rotary(t[:, lo:lo + PAIR_W], tab_ref, rot[ci], rows)
                    qkv_ref[0, rows, c0 + lo:c0 + lo + PAIR_W] = piece.astype(BF16)
            elif c0 < QKV_W:
                qkv_ref[0, rows, c0:c0 + PROJ_CHUNK] = t.astype(BF16)
            else:
                gates_ref[0, rows, c0 - QKV_W:c0 - QKV_W + PROJ_CHUNK] = t.astype(BF16)
            yield

    _staggered(stages(rows) for rows in _row_groups(x_ref.shape[1]))


def _in_proj(x, ada3, ln1_w, tabs, w_in_bf16):
    B, S, D = x.shape
    tm = TOKEN_TILE
    return pl.pallas_call(
        _in_proj_kernel,
        grid=(S // tm, B),
        in_specs=[pl.BlockSpec((1, tm, D), lambda si, b: (b, si, 0)),
                  pl.BlockSpec((1, N_ADA, D), lambda si, b: (b, 0, 0)),
                  pl.BlockSpec((1, D), lambda si, b: (0, 0)),
                  pl.BlockSpec((4, 3, tm, PAIR_W), lambda si, b: (0, 0, si, 0)),
                  pl.BlockSpec((D, IN_WIDTH), lambda si, b: (0, 0),
                               pipeline_mode=pl.Buffered(1))],
        out_specs=[pl.BlockSpec((1, tm, QKV_W), lambda si, b: (b, si, 0)),
                   pl.BlockSpec((1, tm, GATES_W), lambda si, b: (b, si, 0))],
        out_shape=[jax.ShapeDtypeStruct((B, S, QKV_W), BF16),
                   jax.ShapeDtypeStruct((B, S, GATES_W), BF16)],
        compiler_params=pltpu.CompilerParams(vmem_limit_bytes=VMEM_LIMIT),
        name="in_proj",
    )(x, ada3, ln1_w.reshape(1, D), tabs, w_in_bf16)


def _moba_steps(q_ref, k_ref, v_ref, o_ref, vt_scr):
    S = q_ref.shape[1]
    T = MOBA_BLOCK
    nb = S // T
    lane = lax.broadcasted_iota(jnp.int32, (1, PAIR_W), 1)
    head_masks = [(lane >= h * HEAD_DIM) & (lane < (h + 1) * HEAD_DIM) for h in range(2)]
    krow = lax.broadcasted_iota(jnp.int32, (T, 2 * T), 0)
    qcol = lax.broadcasted_iota(jnp.int32, (T, 2 * T), 1) % T
    causal2 = krow <= qcol

    def score_steps(j, st):
        qt = q_ref[0, j * T:(j + 1) * T, :]
        zero = jnp.zeros_like(qt)
        q2 = jnp.concatenate([jnp.where(hm, qt, zero) for hm in head_masks], axis=0)
        st["blocks"], st["cmax"] = [], []
        for n in range(j + 1):
            def step(n=n):
                sn = lax.dot_general(k_ref[0, n * T:(n + 1) * T, :], q2, NT_DIMS,
                                     preferred_element_type=F32)
                if n == j:
                    sn = jnp.where(causal2, sn, -jnp.inf)
                st["blocks"].append(sn)
                st["cmax"].append(jnp.max(sn, axis=0, keepdims=True))
            yield step

    biases = []

    def prologue_steps():
        for n in range(nb):
            vt = v_ref[0, n * T:(n + 1) * T, :].T
            for h in range(2):
                vt_scr[h, n, 0:HEAD_DIM, :] = vt[h * HEAD_DIM:(h + 1) * HEAD_DIM]
                vt_scr[h, n, HEAD_DIM:, :] = jnp.ones((SUM_ROWS, T), BF16)
            if n % 2:
                yield
        kmean = jnp.mean(k_ref[0].astype(F32).reshape(nb, T, PAIR_W), axis=1)
        km_rows = []
        for hm in head_masks:
            kmh = jnp.where(hm, kmean, 0.0)
            hi = kmh.astype(BF16).astype(F32)
            mid = (kmh - hi).astype(BF16).astype(F32)
            km_rows += [hi, mid, kmh - hi - mid]
        km = jnp.concatenate(km_rows, axis=0).astype(BF16)
        gate_terms = lax.dot_general(km, q_ref[0], NT_DIMS,
                                     preferred_element_type=F32)
        yield
        blk = lax.broadcasted_iota(jnp.int32, (nb, S), 0)
        qblk = lax.broadcasted_iota(jnp.int32, (nb, S), 1) // T
        past = blk < qblk
        for h in range(2):
            gt = gate_terms[3 * nb * h:3 * nb * (h + 1)]
            gate = gt[0:nb] + gt[nb:2 * nb] + gt[2 * nb:3 * nb]
            g = jnp.where(past, gate, -jnp.inf)
            rank = jnp.zeros((nb, S), jnp.int32)
            for m in range(nb - 1):
                gm = g[m:m + 1, :]
                beats = (gm > g) | ((gm == g) & (m < blk))
                rank = rank + beats.astype(jnp.int32)
            sel = past & (rank < MOBA_TOPK)
            biases.append(jnp.where(sel, 0.0, -jnp.inf).astype(F32))
            yield

    order = list(range(nb - 1, -1, -1))
    states = [dict() for _ in range(nb)]
    first = [step for j in order[:SCORE_LOOKAHEAD] for step in score_steps(j, states[j])]
    pro = prologue_steps()
    for step in first:
        step()
        next(pro, None)
        yield
    for _ in pro:
        yield

    def value_steps(j, st):
        bias = jnp.concatenate([b[:, j * T:(j + 1) * T] for b in biases], axis=1)
        m = st["cmax"][j]
        for n in range(j):
            m = jnp.maximum(m, st["cmax"][n] + bias[n:n + 1])
        acc = [jnp.zeros((HEAD_DIM + SUM_ROWS, T), F32) for _ in range(2)]
        for n in range(j + 1):
            def step(n=n):
                off = m if n == j else m - bias[n:n + 1]
                p = jnp.exp2((st["blocks"][n] - off).astype(BF16))
                for h in range(2):
                    acc[h] = acc[h] + jnp.dot(vt_scr[h, n], p[:, h * T:(h + 1) * T],
                                              preferred_element_type=F32)
            yield step

        def finish():
            outs = [a[0:HEAD_DIM] * (1.0 / a[HEAD_DIM:HEAD_DIM + 1]) for a in acc]
            o_ref[0, j * T:(j + 1) * T, :] = jnp.concatenate(outs, axis=0).T.astype(BF16)
        yield finish

    for i, j in enumerate(order):
        ahead = i + SCORE_LOOKAHEAD
        nxt = list(score_steps(order[ahead], states[order[ahead]])) if ahead < nb else []
        cur = list(value_steps(j, states[j]))
        for t in range(max(len(nxt), len(cur))):
            if t < len(nxt):
                nxt[t]()
            if t < len(cur):
                cur[t]()
            yield
        states[j].clear()


def _ret_steps(q_ref, k_ref, v_ref, g_ref, gn_ref, dec_ref, zeta_ref, xi_ref, cdec_ref, o_ref):
    S = q_ref.shape[1]
    C = RET_CHUNK
    nc = S // C
    lane = lax.broadcasted_iota(jnp.int32, (1, PAIR_W), 1)
    state = [None, None]
    gains = [gn_ref[:, h * RET_V_DIM:(h + 1) * RET_V_DIM] * math.sqrt(RET_V_DIM) for h in range(2)]
    for n in range(nc):
        rows = slice(n * C, (n + 1) * C)
        q = q_ref[0, rows, :]
        kc = k_ref[0, rows, :]
        kt = kc.T
        for h in range(2):
            hm = (lane >= h * HEAD_DIM) & (lane < (h + 1) * HEAD_DIM)
            cols = slice(h * RET_V_DIM, (h + 1) * RET_V_DIM)
            qh = jnp.where(hm, q, jnp.zeros_like(q))
            vh = v_ref[0, rows, cols]
            s = lax.dot_general(qh, kc, NT_DIMS, preferred_element_type=F32)
            p = s.astype(BF16) * dec_ref[h]
            o = jnp.dot(p, vh, preferred_element_type=F32)
            if n > 0:
                o = o + jnp.dot(qh, state[h].astype(BF16), preferred_element_type=F32) * xi_ref[h]
            if n < nc - 1:
                kz = kt * zeta_ref[h]
                kv = jnp.dot(kz, vh, preferred_element_type=F32)
                state[h] = kv if n == 0 else state[h] * cdec_ref[h] + kv
            yield
            d = o - jnp.sum(o, axis=-1, keepdims=True) * (1.0 / RET_V_DIM)
            r = lax.rsqrt(jnp.sum(d * d, axis=-1, keepdims=True) + RET_V_DIM * EPS)
            y = d * r * gains[h]
            hg = 0.5 * g_ref[0, rows, cols].astype(F32)
            o_ref[0, rows, cols] = (hg * (1.0 + jnp.tanh(hg)) * y).astype(BF16)
            yield


def _moba_kernel(*refs):
    for _ in _moba_steps(*refs):
        pass


def _ret_kernel(*refs):
    for _ in _ret_steps(*refs):
        pass


def _ret_tables():
    C = RET_CHUNK
    log_g = jnp.log(1.0 - jnp.power(2.0, -5.0 - jnp.arange(HEADS, dtype=F32)))
    i = jnp.arange(C, dtype=F32)
    diff = i[:, None] - i[None, :]
    dec = jnp.where(diff >= 0, jnp.exp(jnp.maximum(diff, 0.0)[None] * log_g[:, None, None]), 0.0)
    zeta = jnp.exp((C - 1 - i)[None, :] * log_g[:, None])[:, None, :]
    xi = jnp.exp((i + 1)[None, :] * log_g[:, None])
    xi = jnp.broadcast_to(xi[:, :, None], (HEADS, C, RET_V_DIM))
    cdec = jnp.broadcast_to(jnp.exp(C * log_g)[:, None, None], (HEADS, 1, RET_V_DIM))
    return dec.astype(BF16), zeta.astype(BF16), xi, cdec


def _pair_block(col_off):
    return lambda b, p: (b, 0, col_off // PAIR_W + p)


def _moba(qkv):
    B, S, _ = qkv.shape
    nb = S // MOBA_BLOCK
    return pl.pallas_call(
        _moba_kernel,
        grid=(B, N_PAIRS),
        in_specs=[pl.BlockSpec((1, S, PAIR_W), _pair_block(0)),
                  pl.BlockSpec((1, S, PAIR_W), _pair_block(MOBA_W)),
                  pl.BlockSpec((1, S, PAIR_W), _pair_block(2 * MOBA_W))],
        out_specs=pl.BlockSpec((1, S, PAIR_W), _pair_block(0)),
        out_shape=jax.ShapeDtypeStruct((B, S, MOBA_W), BF16),
        scratch_shapes=[pltpu.VMEM((2, nb, HEAD_DIM + SUM_ROWS, MOBA_BLOCK), BF16)],
        name="moba",
    )(qkv, qkv, qkv)


def _ret(qkv, gates, gn_w):
    B, S, _ = qkv.shape
    C = RET_CHUNK
    v_blk = 5 * MOBA_W // (2 * RET_V_DIM)
    dec, zeta, xi, cdec = _ret_tables()
    return pl.pallas_call(
        _ret_kernel,
        grid=(B, N_PAIRS),
        in_specs=[pl.BlockSpec((1, S, PAIR_W), _pair_block(3 * MOBA_W)),
                  pl.BlockSpec((1, S, PAIR_W), _pair_block(4 * MOBA_W)),
                  pl.BlockSpec((1, S, 2 * RET_V_DIM), lambda b, p: (b, 0, v_blk + p)),
                  pl.BlockSpec((1, S, 2 * RET_V_DIM), lambda b, p: (b, 0, p)),
                  pl.BlockSpec((1, 2 * RET_V_DIM), lambda b, p: (0, p)),
                  pl.BlockSpec((2, C, C), lambda b, p: (p, 0, 0)),
                  pl.BlockSpec((2, 1, C), lambda b, p: (p, 0, 0)),
                  pl.BlockSpec((2, C, RET_V_DIM), lambda b, p: (p, 0, 0)),
                  pl.BlockSpec((2, 1, RET_V_DIM), lambda b, p: (p, 0, 0))],
        out_specs=pl.BlockSpec((1, S, 2 * RET_V_DIM), lambda b, p: (b, 0, p)),
        out_shape=jax.ShapeDtypeStruct((B, S, RET_V_W), BF16),
        name="ret",
    )(qkv, qkv, qkv, gates, gn_w.reshape(1, RET_V_W), dec, zeta, xi, cdec)


def _out_mlp_kernel(x_ref, a_ref, r_ref, ga_ref, gr_ref, ada_ref, ln2_ref, fw_ref,
                    wmo_ref, wro_ref, wout_ref, w1_ref, w2_ref, o_ref):
    gate1 = ada_ref[0, 2:3, :]
    shift2 = ada_ref[0, 3:4, :]
    scale2 = ada_ref[0, 4:5, :]
    gate2 = ada_ref[0, 5:6, :]

    def stages(rows):
        ya = jnp.dot(a_ref[0, rows, :], wmo_ref[...], preferred_element_type=F32)
        yr = jnp.dot(r_ref[0, rows, :], wro_ref[...], preferred_element_type=F32)
        yield
        merged = (_sigmoid(ga_ref[0, rows, :].astype(F32)) * ya
                  + _sigmoid(gr_ref[0, rows, :].astype(F32)) * yr).astype(BF16)
        yield
        mo = jnp.dot(merged, wout_ref[...], preferred_element_type=F32)
        yield
        x1 = x_ref[0, rows, :] + gate1 * mo
        h2 = (_rms_norm(x1, ln2_ref[...]) * (1.0 + scale2) + shift2).astype(BF16)
        yield
        acc = jnp.zeros_like(x1)
        for ci in range(D_FF // FF_CHUNK):
            c0 = ci * FF_CHUNK
            hid = jnp.dot(h2, w1_ref[:, c0:c0 + FF_CHUNK], preferred_element_type=F32)
            yield
            hid = jnp.square(jnp.maximum(hid, 0.0)).astype(BF16)
            yield
            acc = acc + jnp.dot(hid, w2_ref[c0:c0 + FF_CHUNK, :], preferred_element_type=F32)
            yield
        x2 = x1 + gate2 * acc
        o_ref[0, rows, :] = _rms_norm(x2, fw_ref[...])

    _staggered(stages(rows) for rows in _row_groups(x_ref.shape[1]))


def _out_mlp(x, a_out, r_out, gates, ada3, ln2_w, final_w, wmo, wro, wout, w1, w2):
    B, S, D = x.shape
    tm = TOKEN_TILE
    const = functools.partial(pl.BlockSpec, pipeline_mode=pl.Buffered(1))
    return pl.pallas_call(
        _out_mlp_kernel,
        grid=(B, S // tm),
        in_specs=[pl.BlockSpec((1, tm, D), lambda b, si: (b, si, 0)),
                  pl.BlockSpec((1, tm, MOBA_W), lambda b, si: (b, si, 0)),
                  pl.BlockSpec((1, tm, RET_V_W), lambda b, si: (b, si, 0)),
                  pl.BlockSpec((1, tm, D), lambda b, si: (b, si, 1)),
                  pl.BlockSpec((1, tm, D), lambda b, si: (b, si, 2)),
                  pl.BlockSpec((1, N_ADA, D), lambda b, si: (b, 0, 0)),
                  pl.BlockSpec((1, D), lambda b, si: (0, 0)),
                  pl.BlockSpec((1, D), lambda b, si: (0, 0)),
                  const((MOBA_W, D), lambda b, si: (0, 0)),
                  const((RET_V_W, D), lambda b, si: (0, 0)),
                  const((D, D), lambda b, si: (0, 0)),
                  const((D, D_FF), lambda b, si: (0, 0)),
                  const((D_FF, D), lambda b, si: (0, 0))],
        out_specs=pl.BlockSpec((1, tm, D), lambda b, si: (b, si, 0)),
        out_shape=jax.ShapeDtypeStruct((B, S, D), F32),
        compiler_params=pltpu.CompilerParams(vmem_limit_bytes=VMEM_LIMIT),
        name="out_mlp",
    )(x, a_out, r_out, gates, gates, ada3, ln2_w.reshape(1, D), final_w.reshape(1, D),
      wmo, wro, wout, w1, w2)


def _rotary_tables(S, inv_freq, scale):
    lane = jnp.arange(PAIR_W)
    upper = (lane % HEAD_DIM) >= HEAD_DIM // 2
    pos = jnp.arange(S, dtype=F32)
    ang = pos[:, None] * inv_freq[lane % (HEAD_DIM // 2)][None, :]
    cos, sin = jnp.cos(ang), jnp.sin(ang)
    return jnp.stack([cos, jnp.where(upper, sin, 0.0), jnp.where(upper, 0.0, -sin)]) * scale


def kernel(x, c, ln1_w, ln2_w, w_ada, b_ada, w_in, ret_gn_w, w_moba_o, w_ret_o,
           w_out, w_ff1, w_ff2, final_norm_w):
    B, S, D = x.shape
    assert w_in.shape[0] == 1, "final norm is fused into the single layer's MLP kernel"
    moba_inv = 1.0 / (ROPE_THETA ** (jnp.arange(0, HEAD_DIM, 2, dtype=F32) / HEAD_DIM))
    ret_inv = 1.0 / (ROPE_THETA ** jnp.linspace(0.0, 1.0, HEAD_DIM // 2, dtype=F32))
    qk_scale = HEAD_DIM ** -0.5
    tabs = jnp.stack([_rotary_tables(S, moba_inv, qk_scale * LOG2E),
                      _rotary_tables(S, moba_inv, 1.0),
                      _rotary_tables(S, ret_inv, 1.0),
                      _rotary_tables(S, ret_inv, qk_scale)])

    ada3 = _ada(c, w_ada[0], b_ada[0]).reshape(B, N_ADA, D)
    qkv, gates = _in_proj(x, ada3, ln1_w[0], tabs, w_in[0].astype(BF16))
    a_out = _moba(qkv)
    r_out = _ret(qkv, gates, ret_gn_w[0])
    return _out_mlp(x, a_out, r_out, gates, ada3, ln2_w[0], final_norm_w,
                    w_moba_o[0].astype(BF16), w_ret_o[0].astype(BF16), w_out[0].astype(BF16),
                    w_ff1[0].astype(BF16), w_ff2[0].astype(BF16))
```

```python
import functools
import math

import jax
import jax.numpy as jnp
from jax import lax
from jax.experimental import pallas as pl
from jax.experimental.pallas import tpu as pltpu

F32 = jnp.float32
BF16 = jnp.bfloat16

D_MODEL = 1024
N_ADA = 6
HEAD_DIM = 64
HEADS = 8
PAIR_W = 2 * HEAD_DIM
N_PAIRS = HEADS // 2
MOBA_W = HEADS * HEAD_DIM
MOBA_BLOCK = 256
MOBA_TOPK = 3
SCORE_LOOKAHEAD = 2
SUM_ROWS = 16
RET_V_DIM = 128
RET_V_W = HEADS * RET_V_DIM
RET_CHUNK = 256
D_FF = 4 * D_MODEL
FF_CHUNK = 1024
ROPE_THETA = 10000.0
EPS = 1e-6
LOG2E = math.log2(math.e)
QKV_W = 3 * MOBA_W + 2 * MOBA_W + RET_V_W
GATES_W = RET_V_W + 2 * D_MODEL
IN_WIDTH = QKV_W + GATES_W
PROJ_CHUNK = 512
TOKEN_TILE = 512
ROW_GROUP_EIGHTHS = (4, 8)
VMEM_LIMIT = 60 * 1024 * 1024

NT_DIMS = (((1,), (1,)), ((), ()))


def _sigmoid(t):
    return 1.0 / (1.0 + jnp.exp(-t))


def _rms_norm(x, w):
    return x * lax.rsqrt(jnp.mean(x * x, axis=-1, keepdims=True) + EPS) * w


def _ada_kernel(c_ref, w_ref, b_ref, o_ref):
    c = c_ref[...]
    ca = c * _sigmoid(c)
    B = c.shape[0]
    w = w_ref[...]
    c_hi = ca.astype(BF16)
    c_lo = (ca - c_hi.astype(F32)).astype(BF16)
    w_hi = w.astype(BF16)
    w_lo = (w - w_hi.astype(F32)).astype(BF16)
    main = jnp.dot(jnp.concatenate([c_hi, c_lo], axis=0), w_hi, preferred_element_type=F32)
    corr = jnp.dot(c_hi, w_lo, preferred_element_type=F32)
    o_ref[...] = main[:B] + main[B:] + corr + b_ref[...]


def _ada(c, w_ada, b_ada):
    B, D = c.shape
    N = w_ada.shape[1]
    bn = 1024
    return pl.pallas_call(
        _ada_kernel,
        grid=(N // bn,),
        in_specs=[pl.BlockSpec((B, D), lambda j: (0, 0)),
                  pl.BlockSpec((D, bn), lambda j: (0, j)),
                  pl.BlockSpec((1, bn), lambda j: (0, j))],
        out_specs=pl.BlockSpec((B, bn), lambda j: (0, j)),
        out_shape=jax.ShapeDtypeStruct((B, N), F32),
        name="ada",
    )(c, w_ada, b_ada.reshape(1, N))


def _rotary(tk, tab_ref, which, rows):
    return (tk * tab_ref[which, 0, rows, :]
            + pltpu.roll(tk, HEAD_DIM // 2, 1) * tab_ref[which, 1, rows, :]
            + pltpu.roll(tk, PAIR_W - HEAD_DIM // 2, 1) * tab_ref[which, 2, rows, :])


def _staggered(gens):
    gens, live = list(gens), []
    while gens or live:
        if gens:
            live.append(gens.pop(0))
        for gen in list(live):
            if next(gen, StopIteration) is StopIteration:
                live.remove(gen)


def _row_groups(tm):
    bounds = [0] + [tm * f // 8 for f in ROW_GROUP_EIGHTHS]
    return [slice(lo, hi) for lo, hi in zip(bounds[:-1], bounds[1:])]


def _in_proj_kernel(x_ref, ada_ref, ln_ref, tab_ref, w_ref, qkv_ref, gates_ref):
    shift = ada_ref[0, 0:1, :]
    scale = ada_ref[0, 1:2, :]
    rot = {0: 0, 1: 1, 3: 2, 4: 3}

    def stages(rows):
        h = (_rms_norm(x_ref[0, rows, :], ln_ref[...]) * (1.0 + scale) + shift).astype(BF16)
        yield
        for ci in range(IN_WIDTH // PROJ_CHUNK):
            c0 = ci * PROJ_CHUNK
            t = jnp.dot(h, w_ref[:, c0:c0 + PROJ_CHUNK], preferred_element_type=F32)
            if ci in rot:
                for i in range(PROJ_CHUNK // PAIR_W):
                    lo = i * PAIR_W
                    piece = _rotary(t[:, lo:lo + PAIR_W], tab_ref, rot[ci], rows)
                    qkv_ref[0, rows, c0 + lo:c0 + lo + PAIR_W] = piece.astype(BF16)
            elif c0 < QKV_W:
                qkv_ref[0, rows, c0:c0 + PROJ_CHUNK] = t.astype(BF16)
            else:
                gates_ref[0, rows, c0 - QKV_W:c0 - QKV_W + PROJ_CHUNK] = t.astype(BF16)
            yield

    _staggered(stages(rows) for rows in _row_groups(x_ref.shape[1]))


def _in_proj(x, ada3, ln1_w, tabs, w_in_bf16):
    B, S, D = x.shape
    tm = TOKEN_TILE
    return pl.pallas_call(
        _in_proj_kernel,
        grid=(S // tm, B),
        in_specs=[pl.BlockSpec((1, tm, D), lambda si, b: (b, si, 0)),
                  pl.BlockSpec((1, N_ADA, D), lambda si, b: (b, 0, 0)),
                  pl.BlockSpec((1, D), lambda si, b: (0, 0)),
                  pl.BlockSpec((4, 3, tm, PAIR_W), lambda si, b: (0, 0, si, 0)),
                  pl.BlockSpec((D, IN_WIDTH), lambda si, b: (0, 0),
                               pipeline_mode=pl.Buffered(1))],
        out_specs=[pl.BlockSpec((1, tm, QKV_W), lambda si, b: (b, si, 0)),
                   pl.BlockSpec((1, tm, GATES_W), lambda si, b: (b, si, 0))],
        out_shape=[jax.ShapeDtypeStruct((B, S, QKV_W), BF16),
                   jax.ShapeDtypeStruct((B, S, GATES_W), BF16)],
        compiler_params=pltpu.CompilerParams(vmem_limit_bytes=VMEM_LIMIT),
        name="in_proj",
    )(x, ada3, ln1_w.reshape(1, D), tabs, w_in_bf16)


def _moba_steps(q_ref, k_ref, v_ref, o_ref, vt_scr):
    S = q_ref.shape[1]
    T = MOBA_BLOCK
    nb = S // T
    lane = lax.broadcasted_iota(jnp.int32, (1, PAIR_W), 1)
    head_masks = [(lane >= h * HEAD_DIM) & (lane < (h + 1) * HEAD_DIM) for h in range(2)]
    krow = lax.broadcasted_iota(jnp.int32, (T, 2 * T), 0)
    qcol = lax.broadcasted_iota(jnp.int32, (T, 2 * T), 1) % T
    causal2 = krow <= qcol

    def score_steps(j, st):
        qt = q_ref[0, j * T:(j + 1) * T, :]
        zero = jnp.zeros_like(qt)
        q2 = jnp.concatenate([jnp.where(hm, qt, zero) for hm in head_masks], axis=0)
        st["blocks"], st["cmax"] = [], []
        for n in range(j + 1):
            def step(n=n):
                sn = lax.dot_general(k_ref[0, n * T:(n + 1) * T, :], q2, NT_DIMS,
                                     preferred_element_type=F32)
                if n == j:
                    sn = jnp.where(causal2, sn, -jnp.inf)
                sb = sn.astype(BF16)
                st["blocks"].append(sb)
                st["cmax"].append(jnp.max(sb, axis=0, keepdims=True).astype(F32))
            yield step

    biases = []

    def prologue_steps():
        for n in range(nb):
            vt = v_ref[0, n * T:(n + 1) * T, :].T
            for h in range(2):
                vt_scr[h, n, 0:HEAD_DIM, :] = vt[h * HEAD_DIM:(h + 1) * HEAD_DIM]
                vt_scr[h, n, HEAD_DIM:, :] = jnp.ones((SUM_ROWS, T), BF16)
            if n % 2:
                yield
        kmean = jnp.mean(k_ref[0].astype(F32).reshape(nb, T, PAIR_W), axis=1)
        km_rows = []
        for hm in head_masks:
            kmh = jnp.where(hm, kmean, 0.0)
            hi = kmh.astype(BF16).astype(F32)
            mid = (kmh - hi).astype(BF16).astype(F32)
            km_rows += [hi, mid, kmh - hi - mid]
        km = jnp.concatenate(km_rows, axis=0).astype(BF16)
        gate_terms = lax.dot_general(km, q_ref[0], NT_DIMS,
                                     preferred_element_type=F32)
        yield
        blk = lax.broadcasted_iota(jnp.int32, (nb, S), 0)
        qblk = lax.broadcasted_iota(jnp.int32, (nb, S), 1) // T
        past = blk < qblk
        for h in range(2):
            gt = gate_terms[3 * nb * h:3 * nb * (h + 1)]
            gate = gt[0:nb] + gt[nb:2 * nb] + gt[2 * nb:3 * nb]
            g = jnp.where(past, gate, -jnp.inf)
            rank = jnp.zeros((nb, S), jnp.int32)
            for m in range(nb - 1):
                gm = g[m:m + 1, :]
                beats = (gm > g) | ((gm == g) & (m < blk))
                rank = rank + beats.astype(jnp.int32)
            sel = past & (rank < MOBA_TOPK)
            biases.append(jnp.where(sel, 0.0, -jnp.inf).astype(F32))
            yield

    order = list(range(nb - 1, -1, -1))
    states = [dict() for _ in range(nb)]
    first = [step for j in order[:SCORE_LOOKAHEAD] for step in score_steps(j, states[j])]
    pro = prologue_steps()
    for step in first:
        step()
        next(pro, None)
        yield
    for _ in pro:
        yield

    def value_steps(j, st):
        bias = jnp.concatenate([b[:, j * T:(j + 1) * T] for b in biases], axis=1)
        m = st["cmax"][j]
        for n in range(j):
            m = jnp.maximum(m, st["cmax"][n] + bias[n:n + 1])
        acc = [jnp.zeros((HEAD_DIM + SUM_ROWS, T), F32) for _ in range(2)]
        for n in range(j + 1):
            def step(n=n):
                off = m if n == j else m - bias[n:n + 1]
                p = jnp.exp2(st["blocks"][n] - off.astype(BF16))
                for h in range(2):
                    acc[h] = acc[h] + jnp.dot(vt_scr[h, n], p[:, h * T:(h + 1) * T],
                                              preferred_element_type=F32)
            yield step

        def finish():
            outs = [a[0:HEAD_DIM] * (1.0 / a[HEAD_DIM:HEAD_DIM + 1]) for a in acc]
            o_ref[0, j * T:(j + 1) * T, :] = jnp.concatenate(outs, axis=0).T.astype(BF16)
        yield finish

    for i, j in enumerate(order):
        ahead = i + SCORE_LOOKAHEAD
        nxt = list(score_steps(order[ahead], states[order[ahead]])) if ahead < nb else []
        cur = list(value_steps(j, states[j]))
        for t in range(max(len(nxt), len(cur))):
            if t < len(nxt):
                nxt[t]()
            if t < len(cur):
                cur[t]()
            yield
        states[j].clear()


def _ret_steps(q_ref, k_ref, v_ref, g_ref, gn_ref, dec_ref, zeta_ref, xi_ref, cdec_ref, o_ref):
    S = q_ref.shape[1]
    C = RET_CHUNK
    nc = S // C
    lane = lax.broadcasted_iota(jnp.int32, (1, PAIR_W), 1)
    state = [None, None]
    gains = [gn_ref[:, h * RET_V_DIM:(h + 1) * RET_V_DIM] * math.sqrt(RET_V_DIM) for h in range(2)]
    for n in range(nc):
        rows = slice(n * C, (n + 1) * C)
        q = q_ref[0, rows, :]
        kc = k_ref[0, rows, :]
        kt = kc.T
        for h in range(2):
            hm = (lane >= h * HEAD_DIM) & (lane < (h + 1) * HEAD_DIM)
            cols = slice(h * RET_V_DIM, (h + 1) * RET_V_DIM)
            qh = jnp.where(hm, q, jnp.zeros_like(q))
            vh = v_ref[0, rows, cols]
            s = lax.dot_general(qh, kc, NT_DIMS, preferred_element_type=F32)
            p = s.astype(BF16) * dec_ref[h]
            o = jnp.dot(p, vh, preferred_element_type=F32)
            if n > 0:
                o = o + jnp.dot(qh, state[h].astype(BF16), preferred_element_type=F32) * xi_ref[h]
            if n < nc - 1:
                kz = kt * zeta_ref[h]
                kv = jnp.dot(kz, vh, preferred_element_type=F32)
                state[h] = kv if n == 0 else state[h] * cdec_ref[h] + kv
            yield
            d = o - jnp.sum(o, axis=-1, keepdims=True) * (1.0 / RET_V_DIM)
            r = lax.rsqrt(jnp.sum(d * d, axis=-1, keepdims=True) + RET_V_DIM * EPS)
            y = d * r * gains[h]
            hg = 0.5 * g_ref[0, rows, cols].astype(F32)
            o_ref[0, rows, cols] = (hg * (1.0 + jnp.tanh(hg)) * y).astype(BF16)
            yield


def _moba_kernel(*refs):
    for _ in _moba_steps(*refs):
        pass


def _ret_kernel(*refs):
    for _ in _ret_steps(*refs):
        pass


def _ret_tables():
    C = RET_CHUNK
    log_g = jnp.log(1.0 - jnp.power(2.0, -5.0 - jnp.arange(HEADS, dtype=F32)))
    i = jnp.arange(C, dtype=F32)
    diff = i[:, None] - i[None, :]
    dec = jnp.where(diff >= 0, jnp.exp(jnp.maximum(diff, 0.0)[None] * log_g[:, None, None]), 0.0)
    zeta = jnp.exp((C - 1 - i)[None, :] * log_g[:, None])[:, None, :]
    xi = jnp.exp((i + 1)[None, :] * log_g[:, None])
    xi = jnp.broadcast_to(xi[:, :, None], (HEADS, C, RET_V_DIM))
    cdec = jnp.broadcast_to(jnp.exp(C * log_g)[:, None, None], (HEADS, 1, RET_V_DIM))
    return dec.astype(BF16), zeta.astype(BF16), xi, cdec


def _pair_block(col_off):
    return lambda b, p: (b, 0, col_off // PAIR_W + p)


def _moba(qkv):
    B, S, _ = qkv.shape
    nb = S // MOBA_BLOCK
    return pl.pallas_call(
        _moba_kernel,
        grid=(B, N_PAIRS),
        in_specs=[pl.BlockSpec((1, S, PAIR_W), _pair_block(0)),
                  pl.BlockSpec((1, S, PAIR_W), _pair_block(MOBA_W)),
                  pl.BlockSpec((1, S, PAIR_W), _pair_block(2 * MOBA_W))],
        out_specs=pl.BlockSpec((1, S, PAIR_W), _pair_block(0)),
        out_shape=jax.ShapeDtypeStruct((B, S, MOBA_W), BF16),
        scratch_shapes=[pltpu.VMEM((2, nb, HEAD_DIM + SUM_ROWS, MOBA_BLOCK), BF16)],
        name="moba",
    )(qkv, qkv, qkv)


def _ret(qkv, gates, gn_w):
    B, S, _ = qkv.shape
    C = RET_CHUNK
    v_blk = 5 * MOBA_W // (2 * RET_V_DIM)
    dec, zeta, xi, cdec = _ret_tables()
    return pl.pallas_call(
        _ret_kernel,
        grid=(B, N_PAIRS),
        in_specs=[pl.BlockSpec((1, S, PAIR_W), _pair_block(3 * MOBA_W)),
                  pl.BlockSpec((1, S, PAIR_W), _pair_block(4 * MOBA_W)),
                  pl.BlockSpec((1, S, 2 * RET_V_DIM), lambda b, p: (b, 0, v_blk + p)),
                  pl.BlockSpec((1, S, 2 * RET_V_DIM), lambda b, p: (b, 0, p)),
                  pl.BlockSpec((1, 2 * RET_V_DIM), lambda b, p: (0, p)),
                  pl.BlockSpec((2, C, C), lambda b, p: (p, 0, 0)),
                  pl.BlockSpec((2, 1, C), lambda b, p: (p, 0, 0)),
                  pl.BlockSpec((2, C, RET_V_DIM), lambda b, p: (p, 0, 0)),
                  pl.BlockSpec((2, 1, RET_V_DIM), lambda b, p: (p, 0, 0))],
        out_specs=pl.BlockSpec((1, S, 2 * RET_V_DIM), lambda b, p: (b, 0, p)),
        out_shape=jax.ShapeDtypeStruct((B, S, RET_V_W), BF16),
        name="ret",
    )(qkv, qkv, qkv, gates, gn_w.reshape(1, RET_V_W), dec, zeta, xi, cdec)


def _out_mlp_kernel(x_ref, a_ref, r_ref, ga_ref, gr_ref, ada_ref, ln2_ref, fw_ref,
                    wmo_ref, wro_ref, wout_ref, w1_ref, w2_ref, o_ref):
    gate1 = ada_ref[0, 2:3, :]
    shift2 = ada_ref[0, 3:4, :]
    scale2 = ada_ref[0, 4:5, :]
    gate2 = ada_ref[0, 5:6, :]

    def stages(rows):
        ya = jnp.dot(a_ref[0, rows, :], wmo_ref[...], preferred_element_type=F32)
        yr = jnp.dot(r_ref[0, rows, :], wro_ref[...], preferred_element_type=F32)
        yield
        merged = (_sigmoid(ga_ref[0, rows, :].astype(F32)) * ya
                  + _sigmoid(gr_ref[0, rows, :].astype(F32)) * yr).astype(BF16)
        yield
        mo = jnp.dot(merged, wout_ref[...], preferred_element_type=F32)
        yield
        x1 = x_ref[0, rows, :] + gate1 * mo
        h2 = (_rms_norm(x1, ln2_ref[...]) * (1.0 + scale2) + shift2).astype(BF16)
        yield
        acc = jnp.zeros_like(x1)
        for ci in range(D_FF // FF_CHUNK):
            c0 = ci * FF_CHUNK
            hid = jnp.dot(h2, w1_ref[:, c0:c0 + FF_CHUNK], preferred_element_type=F32)
            yield
            hid = jnp.square(jnp.maximum(hid, 0.0)).astype(BF16)
            yield
            acc = acc + jnp.dot(hid, w2_ref[c0:c0 + FF_CHUNK, :], preferred_element_type=F32)
            yield
        x2 = x1 + gate2 * acc
        o_ref[0, rows, :] = _rms_norm(x2, fw_ref[...])

    _staggered(stages(rows) for rows in _row_groups(x_ref.shape[1]))


def _out_mlp(x, a_out, r_out, gates, ada3, ln2_w, final_w, wmo, wro, wout, w1, w2):
    B, S, D = x.shape
    tm = TOKEN_TILE
    const = functools.partial(pl.BlockSpec, pipeline_mode=pl.Buffered(1))
    return pl.pallas_call(
        _out_mlp_kernel,
        grid=(B, S // tm),
        in_specs=[pl.BlockSpec((1, tm, D), lambda b, si: (b, si, 0)),
                  pl.BlockSpec((1, tm, MOBA_W), lambda b, si: (b, si, 0)),
                  pl.BlockSpec((1, tm, RET_V_W), lambda b, si: (b, si, 0)),
                  pl.BlockSpec((1, tm, D), lambda b, si: (b, si, 1)),
                  pl.BlockSpec((1, tm, D), lambda b, si: (b, si, 2)),
                  pl.BlockSpec((1, N_ADA, D), lambda b, si: (b, 0, 0)),
                  pl.BlockSpec((1, D), lambda b, si: (0, 0)),
                  pl.BlockSpec((1, D), lambda b, si: (0, 0)),
                  const((MOBA_W, D), lambda b, si: (0, 0)),
                  const((RET_V_W, D), lambda b, si: (0, 0)),
                  const((D, D), lambda b, si: (0, 0)),
                  const((D, D_FF), lambda b, si: (0, 0)),
                  const((D_FF, D), lambda b, si: (0, 0))],
        out_specs=pl.BlockSpec((1, tm, D), lambda b, si: (b, si, 0)),
        out_shape=jax.ShapeDtypeStruct((B, S, D), F32),
        compiler_params=pltpu.CompilerParams(vmem_limit_bytes=VMEM_LIMIT),
        name="out_mlp",
    )(x, a_out, r_out, gates, gates, ada3, ln2_w.reshape(1, D), final_w.reshape(1, D),
      wmo, wro, wout, w1, w2)


def _rotary_tables(S, inv_freq, scale):
    lane = jnp.arange(PAIR_W)
    upper = (lane % HEAD_DIM) >= HEAD_DIM // 2
    pos = jnp.arange(S, dtype=F32)
    ang = pos[:, None] * inv_freq[lane % (HEAD_DIM // 2)][None, :]
    cos, sin = jnp.cos(ang), jnp.sin(ang)
    return jnp.stack([cos, jnp.where(upper, sin, 0.0), jnp.where(upper, 0.0, -sin)]) * scale


def kernel(x, c, ln1_w, ln2_w, w_ada, b_ada, w_in, ret_gn_w, w_moba_o, w_ret_o,
           w_out, w_ff1, w_ff2, final_norm_w):
    B, S, D = x.shape
    assert w_in.shape[0] == 1, "final norm is fused into the single layer's MLP kernel"
    moba_inv = 1.0 / (ROPE_THETA ** (jnp.arange(0, HEAD_DIM, 2, dtype=F32) / HEAD_DIM))
    ret_inv = 1.0 / (ROPE_THETA ** jnp.linspace(0.0, 1.0, HEAD_DIM // 2, dtype=F32))
    qk_scale = HEAD_DIM ** -0.5
    tabs = jnp.stack([_rotary_tables(S, moba_inv, qk_scale * LOG2E),
                      _rotary_tables(S, moba_inv, 1.0),
                      _rotary_tables(S, ret_inv, 1.0),
                      _rotary_tables(S, ret_inv, qk_scale)])

    ada3 = _ada(c, w_ada[0], b_ada[0]).reshape(B, N_ADA, D)
    qkv, gates = _in_proj(x, ada3, ln1_w[0], tabs, w_in[0].astype(BF16))
    a_out = _moba(qkv)
    r_out = _ret(qkv, gates, ret_gn_w[0])
    return _out_mlp(x, a_out, r_out, gates, ada3, ln2_w[0], final_norm_w,
                    w_moba_o[0].astype(BF16), w_ret_o[0].astype(BF16), w_out[0].astype(BF16),
                    w_ff1[0].astype(BF16), w_ff2[0].astype(BF16))
```

```python
import functools
import math

import jax
import jax.numpy as jnp
from jax import lax
from jax.experimental import pallas as pl
from jax.experimental.pallas import tpu as pltpu

F32 = jnp.float32
BF16 = jnp.bfloat16

D_MODEL = 1024
N_ADA = 6
HEAD_DIM = 64
HEADS = 8
PAIR_W = 2 * HEAD_DIM
N_PAIRS = HEADS // 2
MOBA_W = HEADS * HEAD_DIM
MOBA_BLOCK = 256
MOBA_TOPK = 3
SCORE_LOOKAHEAD = 1
SUM_ROWS = 16
RET_V_DIM = 128
RET_V_W = HEADS * RET_V_DIM
RET_CHUNK = 256
D_FF = 4 * D_MODEL
FF_CHUNK = 1024
ROPE_THETA = 10000.0
EPS = 1e-6
LOG2E = math.log2(math.e)
QKV_W = 3 * MOBA_W + 2 * MOBA_W + RET_V_W
GATES_W = RET_V_W + 2 * D_MODEL
IN_WIDTH = QKV_W + GATES_W
PROJ_CHUNK = 512
TOKEN_TILE = 512
ROW_GROUP_EIGHTHS = (4, 8)
VMEM_LIMIT = 60 * 1024 * 1024

NT_DIMS = (((1,), (1,)), ((), ()))


def _sigmoid(t):
    return 1.0 / (1.0 + jnp.exp(-t))


def _rms_norm(x, w):
    return x * lax.rsqrt(jnp.mean(x * x, axis=-1, keepdims=True) + EPS) * w


def _ada_kernel(c_ref, w_ref, b_ref, o_ref):
    c = c_ref[...]
    ca = c * _sigmoid(c)
    B = c.shape[0]
    w = w_ref[...]
    c_hi = ca.astype(BF16)
    c_lo = (ca - c_hi.astype(F32)).astype(BF16)
    w_hi = w.astype(BF16)
    w_lo = (w - w_hi.astype(F32)).astype(BF16)
    main = jnp.dot(jnp.concatenate([c_hi, c_lo], axis=0), w_hi, preferred_element_type=F32)
    corr = jnp.dot(c_hi, w_lo, preferred_element_type=F32)
    o_ref[...] = main[:B] + main[B:] + corr + b_ref[...]


def _ada(c, w_ada, b_ada):
    B, D = c.shape
    N = w_ada.shape[1]
    bn = 1024
    return pl.pallas_call(
        _ada_kernel,
        grid=(N // bn,),
        in_specs=[pl.BlockSpec((B, D), lambda j: (0, 0)),
                  pl.BlockSpec((D, bn), lambda j: (0, j)),
                  pl.BlockSpec((1, bn), lambda j: (0, j))],
        out_specs=pl.BlockSpec((B, bn), lambda j: (0, j)),
        out_shape=jax.ShapeDtypeStruct((B, N), F32),
        name="ada",
    )(c, w_ada, b_ada.reshape(1, N))


def _rotary(tk, tab_ref, which, rows):
    return (tk * tab_ref[which, 0, rows, :]
            + pltpu.roll(tk, HEAD_DIM // 2, 1) * tab_ref[which, 1, rows, :]
            + pltpu.roll(tk, PAIR_W - HEAD_DIM // 2, 1) * tab_ref[which, 2, rows, :])


def _staggered(gens):
    gens, live = list(gens), []
    while gens or live:
        if gens:
            live.append(gens.pop(0))
        for gen in list(live):
            if next(gen, StopIteration) is StopIteration:
                live.remove(gen)


def _row_groups(tm):
    bounds = [0] + [tm * f // 8 for f in ROW_GROUP_EIGHTHS]
    return [slice(lo, hi) for lo, hi in zip(bounds[:-1], bounds[1:])]


def _in_proj_kernel(x_ref, ada_ref, ln_ref, tab_ref, w_ref, qkv_ref, gates_ref):
    shift = ada_ref[0, 0:1, :]
    scale = ada_ref[0, 1:2, :]
    rot = {0: 0, 1: 1, 3: 2, 4: 3}

    def stages(rows):
        h = (_rms_norm(x_ref[0, rows, :], ln_ref[...]) * (1.0 + scale) + shift).astype(BF16)
        yield
        for ci in range(IN_WIDTH // PROJ_CHUNK):
            c0 = ci * PROJ_CHUNK
            t = jnp.dot(h, w_ref[:, c0:c0 + PROJ_CHUNK], preferred_element_type=F32)
            if ci in rot:
                for i in range(PROJ_CHUNK // PAIR_W):
                    lo = i * PAIR_W
                    piece = _rotary(t[:, lo:lo + PAIR_W], tab_ref, rot[ci], rows)
                    qkv_ref[0, rows, c0 + lo:c0 + lo + PAIR_W] = piece.astype(BF16)
            elif c0 < QKV_W:
                qkv_ref[0, rows, c0:c0 + PROJ_CHUNK] = t.astype(BF16)
            else:
                gates_ref[0, rows, c0 - QKV_W:c0 - QKV_W + PROJ_CHUNK] = t.astype(BF16)
            yield

    _staggered(stages(rows) for rows in _row_groups(x_ref.shape[1]))


def _in_proj(x, ada3, ln1_w, tabs, w_in_bf16):
    B, S, D = x.shape
    tm = TOKEN_TILE
    return pl.pallas_call(
        _in_proj_kernel,
        grid=(S // tm, B),
        in_specs=[pl.BlockSpec((1, tm, D), lambda si, b: (b, si, 0)),
                  pl.BlockSpec((1, N_ADA, D), lambda si, b: (b, 0, 0)),
                  pl.BlockSpec((1, D), lambda si, b: (0, 0)),
                  pl.BlockSpec((4, 3, tm, PAIR_W), lambda si, b: (0, 0, si, 0)),
                  pl.BlockSpec((D, IN_WIDTH), lambda si, b: (0, 0),
                               pipeline_mode=pl.Buffered(1))],
        out_specs=[pl.BlockSpec((1, tm, QKV_W), lambda si, b: (b, si, 0)),
                   pl.BlockSpec((1, tm, GATES_W), lambda si, b: (b, si, 0))],
        out_shape=[jax.ShapeDtypeStruct((B, S, QKV_W), BF16),
                   jax.ShapeDtypeStruct((B, S, GATES_W), BF16)],
        compiler_params=pltpu.CompilerParams(vmem_limit_bytes=VMEM_LIMIT),
        name="in_proj",
    )(x, ada3, ln1_w.reshape(1, D), tabs, w_in_bf16)


def _moba_steps(q_ref, k_ref, v_ref, o_ref, vt_scr):
    S = q_ref.shape[1]
    T = MOBA_BLOCK
    nb = S // T
    lane = lax.broadcasted_iota(jnp.int32, (1, PAIR_W), 1)
    head_masks = [(lane >= h * HEAD_DIM) & (lane < (h + 1) * HEAD_DIM) for h in range(2)]
    krow = lax.broadcasted_iota(jnp.int32, (T, 2 * T), 0)
    qcol = lax.broadcasted_iota(jnp.int32, (T, 2 * T), 1) % T
    causal2 = krow <= qcol

    def score_steps(j, st):
        qt = q_ref[0, j * T:(j + 1) * T, :]
        zero = jnp.zeros_like(qt)
        q2 = jnp.concatenate([jnp.where(hm, qt, zero) for hm in head_masks], axis=0)
        st["blocks"], st["cmax"] = [], []
        for n in range(j + 1):
            def step(n=n):
                sn = lax.dot_general(k_ref[0, n * T:(n + 1) * T, :], q2, NT_DIMS,
                                     preferred_element_type=F32)
                if n == j:
                    sn = jnp.where(causal2, sn, -jnp.inf)
                sb = sn.astype(BF16)
                st["blocks"].append(sb)
                st["cmax"].append(jnp.max(sb, axis=0, keepdims=True).astype(F32))
            yield step

    biases = []

    def prologue_steps():
        for n in range(nb):
            vt = v_ref[0, n * T:(n + 1) * T, :].T
            for h in range(2):
                vt_scr[h, n, 0:HEAD_DIM, :] = vt[h * HEAD_DIM:(h + 1) * HEAD_DIM]
                vt_scr[h, n, HEAD_DIM:, :] = jnp.ones((SUM_ROWS, T), BF16)
            if n % 2:
                yield
        kmean = jnp.mean(k_ref[0].astype(F32).reshape(nb, T, PAIR_W), axis=1)
        km_rows = []
        for hm in head_masks:
            kmh = jnp.where(hm, kmean, 0.0)
            hi = kmh.astype(BF16).astype(F32)
            mid = (kmh - hi).astype(BF16).astype(F32)
            km_rows += [hi, mid, kmh - hi - mid]
        km = jnp.concatenate(km_rows, axis=0).astype(BF16)
        gate_terms = lax.dot_general(km, q_ref[0], NT_DIMS,
                                     preferred_element_type=F32)
        yield
        blk = lax.broadcasted_iota(jnp.int32, (nb, S), 0)
        qblk = lax.broadcasted_iota(jnp.int32, (nb, S), 1) // T
        past = blk < qblk
        for h in range(2):
            gt = gate_terms[3 * nb * h:3 * nb * (h + 1)]
            gate = gt[0:nb] + gt[nb:2 * nb] + gt[2 * nb:3 * nb]
            g = jnp.where(past, gate, -jnp.inf)
            rank = jnp.zeros((nb, S), jnp.int32)
            for m in range(nb - 1):
                gm = g[m:m + 1, :]
                beats = (gm > g) | ((gm == g) & (m < blk))
                rank = rank + beats.astype(jnp.int32)
            sel = past & (rank < MOBA_TOPK)
            biases.append(jnp.where(sel, 0.0, -jnp.inf).astype(F32))
            yield

    order = list(range(nb - 1, -1, -1))
    states = [dict() for _ in range(nb)]
    first = [step for j in order[:SCORE_LOOKAHEAD] for step in score_steps(j, states[j])]
    pro = prologue_steps()
    for step in first:
        step()
        next(pro, None)
        yield
    for _ in pro:
        yield

    def value_steps(j, st):
        bias = jnp.concatenate([b[:, j * T:(j + 1) * T] for b in biases], axis=1)
        m = st["cmax"][j]
        for n in range(j):
            m = jnp.maximum(m, st["cmax"][n] + bias[n:n + 1])
        acc = [jnp.zeros((HEAD_DIM + SUM_ROWS, T), F32) for _ in range(2)]
        for n in range(j + 1):
            def step(n=n):
                off = m if n == j else m - bias[n:n + 1]
                p = jnp.exp2(st["blocks"][n] - off.astype(BF16))
                for h in range(2):
                    acc[h] = acc[h] + jnp.dot(vt_scr[h, n], p[:, h * T:(h + 1) * T],
                                              preferred_element_type=F32)
            yield step

        def finish():
            outs = [a[0:HEAD_DIM] * (1.0 / a[HEAD_DIM:HEAD_DIM + 1]) for a in acc]
            o_ref[0, j * T:(j + 1) * T, :] = jnp.concatenate(outs, axis=0).T.astype(BF16)
        yield finish

    for i, j in enumerate(order):
        ahead = i + SCORE_LOOKAHEAD
        nxt = list(score_steps(order[ahead], states[order[ahead]])) if ahead < nb else []
        cur = list(value_steps(j, states[j]))
        for t in range(max(len(nxt), len(cur))):
            if t < len(nxt):
                nxt[t]()
            if t < len(cur):
                cur[t]()
            yield
        states[j].clear()


def _ret_steps(q_ref, k_ref, v_ref, g_ref, gn_ref, dec_ref, zeta_ref, xi_ref, cdec_ref, o_ref):
    S = q_ref.shape[1]
    C = RET_CHUNK
    nc = S // C
    lane = lax.broadcasted_iota(jnp.int32, (1, PAIR_W), 1)
    state = [None, None]
    gains = [gn_ref[:, h * RET_V_DIM:(h + 1) * RET_V_DIM] * math.sqrt(RET_V_DIM) for h in range(2)]
    for n in range(nc):
        rows = slice(n * C, (n + 1) * C)
        q = q_ref[0, rows, :]
        kc = k_ref[0, rows, :]
        kt = kc.T
        for h in range(2):
            hm = (lane >= h * HEAD_DIM) & (lane < (h + 1) * HEAD_DIM)
            cols = slice(h * RET_V_DIM, (h + 1) * RET_V_DIM)
            qh = jnp.where(hm, q, jnp.zeros_like(q))
            vh = v_ref[0, rows, cols]
            s = lax.dot_general(qh, kc, NT_DIMS, preferred_element_type=F32)
            p = s.astype(BF16) * dec_ref[h]
            o = jnp.dot(p, vh, preferred_element_type=F32)
            if n > 0:
                o = o + jnp.dot(qh, state[h].astype(BF16), preferred_element_type=F32) * xi_ref[h]
            if n < nc - 1:
                kz = kt * zeta_ref[h]
                kv = jnp.dot(kz, vh, preferred_element_type=F32)
                state[h] = kv if n == 0 else state[h] * cdec_ref[h] + kv
            yield
            d = o - jnp.sum(o, axis=-1, keepdims=True) * (1.0 / RET_V_DIM)
            r = lax.rsqrt(jnp.sum(d * d, axis=-1, keepdims=True) + RET_V_DIM * EPS)
            y = d * r * gains[h]
            hg = 0.5 * g_ref[0, rows, cols].astype(F32)
            o_ref[0, rows, cols] = (hg * (1.0 + jnp.tanh(hg)) * y).astype(BF16)
            yield


def _moba_kernel(*refs):
    for _ in _moba_steps(*refs):
        pass


def _ret_kernel(*refs):
    for _ in _ret_steps(*refs):
        pass


def _ret_tables():
    C = RET_CHUNK
    log_g = jnp.log(1.0 - jnp.power(2.0, -5.0 - jnp.arange(HEADS, dtype=F32)))
    i = jnp.arange(C, dtype=F32)
    diff = i[:, None] - i[None, :]
    dec = jnp.where(diff >= 0, jnp.exp(jnp.maximum(diff, 0.0)[None] * log_g[:, None, None]), 0.0)
    zeta = jnp.exp((C - 1 - i)[None, :] * log_g[:, None])[:, None, :]
    xi = jnp.exp((i + 1)[None, :] * log_g[:, None])
    xi = jnp.broadcast_to(xi[:, :, None], (HEADS, C, RET_V_DIM))
    cdec = jnp.broadcast_to(jnp.exp(C * log_g)[:, None, None], (HEADS, 1, RET_V_DIM))
    return dec.astype(BF16), zeta.astype(BF16), xi, cdec


def _pair_block(col_off):
    return lambda b, p: (b, 0, col_off // PAIR_W + p)


def _moba(qkv):
    B, S, _ = qkv.shape
    nb = S // MOBA_BLOCK
    return pl.pallas_call(
        _moba_kernel,
        grid=(B, N_PAIRS),
        in_specs=[pl.BlockSpec((1, S, PAIR_W), _pair_block(0)),
                  pl.BlockSpec((1, S, PAIR_W), _pair_block(MOBA_W)),
                  pl.BlockSpec((1, S, PAIR_W), _pair_block(2 * MOBA_W))],
        out_specs=pl.BlockSpec((1, S, PAIR_W), _pair_block(0)),
        out_shape=jax.ShapeDtypeStruct((B, S, MOBA_W), BF16),
        scratch_shapes=[pltpu.VMEM((2, nb, HEAD_DIM + SUM_ROWS, MOBA_BLOCK), BF16)],
        name="moba",
    )(qkv, qkv, qkv)


def _ret(qkv, gates, gn_w):
    B, S, _ = qkv.shape
    C = RET_CHUNK
    v_blk = 5 * MOBA_W // (2 * RET_V_DIM)
    dec, zeta, xi, cdec = _ret_tables()
    return pl.pallas_call(
        _ret_kernel,
        grid=(B, N_PAIRS),
        in_specs=[pl.BlockSpec((1, S, PAIR_W), _pair_block(3 * MOBA_W)),
                  pl.BlockSpec((1, S, PAIR_W), _pair_block(4 * MOBA_W)),
                  pl.BlockSpec((1, S, 2 * RET_V_DIM), lambda b, p: (b, 0, v_blk + p)),
                  pl.BlockSpec((1, S, 2 * RET_V_DIM), lambda b, p: (b, 0, p)),
                  pl.BlockSpec((1, 2 * RET_V_DIM), lambda b, p: (0, p)),
                  pl.BlockSpec((2, C, C), lambda b, p: (p, 0, 0)),
                  pl.BlockSpec((2, 1, C), lambda b, p: (p, 0, 0)),
                  pl.BlockSpec((2, C, RET_V_DIM), lambda b, p: (p, 0, 0)),
                  pl.BlockSpec((2, 1, RET_V_DIM), lambda b, p: (p, 0, 0))],
        out_specs=pl.BlockSpec((1, S, 2 * RET_V_DIM), lambda b, p: (b, 0, p)),
        out_shape=jax.ShapeDtypeStruct((B, S, RET_V_W), BF16),
        name="ret",
    )(qkv, qkv, qkv, gates, gn_w.reshape(1, RET_V_W), dec, zeta, xi, cdec)


def _out_mlp_kernel(x_ref, a_ref, r_ref, ga_ref, gr_ref, ada_ref, ln2_ref, fw_ref,
                    wmo_ref, wro_ref, wout_ref, w1_ref, w2_ref, o_ref):
    gate1 = ada_ref[0, 2:3, :]
    shift2 = ada_ref[0, 3:4, :]
    scale2 = ada_ref[0, 4:5, :]
    gate2 = ada_ref[0, 5:6, :]

    def stages(rows):
        ya = jnp.dot(a_ref[0, rows, :], wmo_ref[...], preferred_element_type=F32)
        yr = jnp.dot(r_ref[0, rows, :], wro_ref[...], preferred_element_type=F32)
        yield
        merged = (_sigmoid(ga_ref[0, rows, :].astype(F32)) * ya
                  + _sigmoid(gr_ref[0, rows, :].astype(F32)) * yr).astype(BF16)
        yield
        mo = jnp.dot(merged, wout_ref[...], preferred_element_type=F32)
        yield
        x1 = x_ref[0, rows, :] + gate1 * mo
        h2 = (_rms_norm(x1, ln2_ref[...]) * (1.0 + scale2) + shift2).astype(BF16)
        yield
        acc = jnp.zeros_like(x1)
        for ci in range(D_FF // FF_CHUNK):
            c0 = ci * FF_CHUNK
            hid = jnp.dot(h2, w1_ref[:, c0:c0 + FF_CHUNK], preferred_element_type=F32)
            yield
            hid = jnp.square(jnp.maximum(hid, 0.0)).astype(BF16)
            yield
            acc = acc + jnp.dot(hid, w2_ref[c0:c0 + FF_CHUNK, :], preferred_element_type=F32)
            yield
        x2 = x1 + gate2 * acc
        o_ref[0, rows, :] = _rms_norm(x2, fw_ref[...])

    _staggered(stages(rows) for rows in _row_groups(x_ref.shape[1]))


def _out_mlp(x, a_out, r_out, gates, ada3, ln2_w, final_w, wmo, wro, wout, w1, w2):
    B, S, D = x.shape
    tm = TOKEN_TILE
    const = functools.partial(pl.BlockSpec, pipeline_mode=pl.Buffered(1))
    return pl.pallas_call(
        _out_mlp_kernel,
        grid=(B, S // tm),
        in_specs=[pl.BlockSpec((1, tm, D), lambda b, si: (b, si, 0)),
                  pl.BlockSpec((1, tm, MOBA_W), lambda b, si: (b, si, 0)),
                  pl.BlockSpec((1, tm, RET_V_W), lambda b, si: (b, si, 0)),
                  pl.BlockSpec((1, tm, D), lambda b, si: (b, si, 1)),
                  pl.BlockSpec((1, tm, D), lambda b, si: (b, si, 2)),
                  pl.BlockSpec((1, N_ADA, D), lambda b, si: (b, 0, 0)),
                  pl.BlockSpec((1, D), lambda b, si: (0, 0)),
                  pl.BlockSpec((1, D), lambda b, si: (0, 0)),
                  const((MOBA_W, D), lambda b, si: (0, 0)),
                  const((RET_V_W, D), lambda b, si: (0, 0)),
                  const((D, D), lambda b, si: (0, 0)),
                  const((D, D_FF), lambda b, si: (0, 0)),
                  const((D_FF, D), lambda b, si: (0, 0))],
        out_specs=pl.BlockSpec((1, tm, D), lambda b, si: (b, si, 0)),
        out_shape=jax.ShapeDtypeStruct((B, S, D), F32),
        compiler_params=pltpu.CompilerParams(vmem_limit_bytes=VMEM_LIMIT),
        name="out_mlp",
    )(x, a_out, r_out, gates, gates, ada3, ln2_w.reshape(1, D), final_w.reshape(1, D),
      wmo, wro, wout, w1, w2)


def _rotary_tables(S, inv_freq, scale):
    lane = jnp.arange(PAIR_W)
    upper = (lane % HEAD_DIM) >= HEAD_DIM // 2
    pos = jnp.arange(S, dtype=F32)
    ang = pos[:, None] * inv_freq[lane % (HEAD_DIM // 2)][None, :]
    cos, sin = jnp.cos(ang), jnp.sin(ang)
    return jnp.stack([cos, jnp.where(upper, sin, 0.0), jnp.where(upper, 0.0, -sin)]) * scale


def kernel(x, c, ln1_w, ln2_w, w_ada, b_ada, w_in, ret_gn_w, w_moba_o, w_ret_o,
           w_out, w_ff1, w_ff2, final_norm_w):
    B, S, D = x.shape
    assert w_in.shape[0] == 1, "final norm is fused into the single layer's MLP kernel"
    moba_inv = 1.0 / (ROPE_THETA ** (jnp.arange(0, HEAD_DIM, 2, dtype=F32) / HEAD_DIM))
    ret_inv = 1.0 / (ROPE_THETA ** jnp.linspace(0.0, 1.0, HEAD_DIM // 2, dtype=F32))
    qk_scale = HEAD_DIM ** -0.5
    tabs = jnp.stack([_rotary_tables(S, moba_inv, qk_scale * LOG2E),
                      _rotary_tables(S, moba_inv, 1.0),
                      _rotary_tables(S, ret_inv, 1.0),
                      _rotary_tables(S, ret_inv, qk_scale)])

    ada3 = _ada(c, w_ada[0], b_ada[0]).reshape(B, N_ADA, D)
    qkv, gates = _in_proj(x, ada3, ln1_w[0], tabs, w_in[0].astype(BF16))
    a_out = _moba(qkv)
    r_out = _ret(qkv, gates, ret_gn_w[0])
    return _out_mlp(x, a_out, r_out, gates, ada3, ln2_w[0], final_norm_w,
                    w_moba_o[0].astype(BF16), w_ret_o[0].astype(BF16), w_out[0].astype(BF16),
                    w_ff1[0].astype(BF16), w_ff2[0].astype(BF16))
```

```python
import functools
import math

import jax
import jax.numpy as jnp
from jax import lax
from jax.experimental import pallas as pl
from jax.experimental.pallas import tpu as pltpu

F32 = jnp.float32
BF16 = jnp.bfloat16

D_MODEL = 1024
N_ADA = 6
HEAD_DIM = 64
HEADS = 8
PAIR_W = 2 * HEAD_DIM
N_PAIRS = HEADS // 2
MOBA_W = HEADS * HEAD_DIM
MOBA_BLOCK = 256
MOBA_TOPK = 3
SCORE_LOOKAHEAD = 1
SUM_ROWS = 16
RET_V_DIM = 128
RET_V_W = HEADS * RET_V_DIM
RET_CHUNK = 256
D_FF = 4 * D_MODEL
FF_CHUNK = 1024
ROPE_THETA = 10000.0
EPS = 1e-6
LOG2E = math.log2(math.e)
QKV_W = 3 * MOBA_W + 2 * MOBA_W + RET_V_W
GATES_W = RET_V_W + 2 * D_MODEL
IN_WIDTH = QKV_W + GATES_W
PROJ_CHUNK = 512
TOKEN_TILE = 512
ROW_GROUP_EIGHTHS = (4, 8)
VMEM_LIMIT = 60 * 1024 * 1024

NT_DIMS = (((1,), (1,)), ((), ()))


def _sigmoid(t):
    return 1.0 / (1.0 + jnp.exp(-t))


def _rms_norm(x, w):
    return x * lax.rsqrt(jnp.mean(x * x, axis=-1, keepdims=True) + EPS) * w


def _ada_kernel(c_ref, w_ref, b_ref, o_ref):
    c = c_ref[...]
    ca = c * _sigmoid(c)
    B = c.shape[0]
    w = w_ref[...]
    c_hi = ca.astype(BF16)
    c_lo = (ca - c_hi.astype(F32)).astype(BF16)
    w_hi = w.astype(BF16)
    w_lo = (w - w_hi.astype(F32)).astype(BF16)
    main = jnp.dot(jnp.concatenate([c_hi, c_lo], axis=0), w_hi, preferred_element_type=F32)
    corr = jnp.dot(c_hi, w_lo, preferred_element_type=F32)
    o_ref[...] = main[:B] + main[B:] + corr + b_ref[...]


def _ada(c, w_ada, b_ada):
    B, D = c.shape
    N = w_ada.shape[1]
    bn = 1024
    return pl.pallas_call(
        _ada_kernel,
        grid=(N // bn,),
        in_specs=[pl.BlockSpec((B, D), lambda j: (0, 0)),
                  pl.BlockSpec((D, bn), lambda j: (0, j)),
                  pl.BlockSpec((1, bn), lambda j: (0, j))],
        out_specs=pl.BlockSpec((B, bn), lambda j: (0, j)),
        out_shape=jax.ShapeDtypeStruct((B, N), F32),
        name="ada",
    )(c, w_ada, b_ada.reshape(1, N))


def _rotary(tk, tab_ref, which, rows):
    return (tk * tab_ref[which, 0, rows, :]
            + pltpu.roll(tk, HEAD_DIM // 2, 1) * tab_ref[which, 1, rows, :]
            + pltpu.roll(tk, PAIR_W - HEAD_DIM // 2, 1) * tab_ref[which, 2, rows, :])


def _staggered(gens):
    gens, live = list(gens), []
    while gens or live:
        if gens:
            live.append(gens.pop(0))
        for gen in list(live):
            if next(gen, StopIteration) is StopIteration:
                live.remove(gen)


def _row_groups(tm):
    bounds = [0] + [tm * f // 8 for f in ROW_GROUP_EIGHTHS]
    return [slice(lo, hi) for lo, hi in zip(bounds[:-1], bounds[1:])]


def _in_proj_kernel(x_ref, ada_ref, ln_ref, tab_ref, w_ref, qkv_ref, gates_ref):
    shift = ada_ref[0, 0:1, :]
    scale = ada_ref[0, 1:2, :]
    rot = {0: 0, 1: 1, 3: 2, 4: 3}

    def stages(rows):
        h = (_rms_norm(x_ref[0, rows, :], ln_ref[...]) * (1.0 + scale) + shift).astype(BF16)
        yield
        for ci in range(IN_WIDTH // PROJ_CHUNK):
            c0 = ci * PROJ_CHUNK
            t = jnp.dot(h, w_ref[:, c0:c0 + PROJ_CHUNK], preferred_element_type=F32)
            if ci in rot:
                for i in range(PROJ_CHUNK // PAIR_W):
                    lo = i * PAIR_W
                    piece = _rotary(t[:, lo:lo + PAIR_W], tab_ref, rot[ci], rows)
                    qkv_ref[0, rows, c0 + lo:c0 + lo + PAIR_W] = piece.astype(BF16)
            elif c0 < QKV_W:
                qkv_ref[0, rows, c0:c0 + PROJ_CHUNK] = t.astype(BF16)
            else:
                gates_ref[0, rows, c0 - QKV_W:c0 - QKV_W + PROJ_CHUNK] = t.astype(BF16)
            yield

    _staggered(stages(rows) for rows in _row_groups(x_ref.shape[1]))


def _in_proj(x, ada3, ln1_w, tabs, w_in_bf16):
    B, S, D = x.shape
    tm = TOKEN_TILE
    return pl.pallas_call(
        _in_proj_kernel,
        grid=(S // tm, B),
        in_specs=[pl.BlockSpec((1, tm, D), lambda si, b: (b, si, 0)),
                  pl.BlockSpec((1, N_ADA, D), lambda si, b: (b, 0, 0)),
                  pl.BlockSpec((1, D), lambda si, b: (0, 0)),
                  pl.BlockSpec((4, 3, tm, PAIR_W), lambda si, b: (0, 0, si, 0)),
                  pl.BlockSpec((D, IN_WIDTH), lambda si, b: (0, 0),
                               pipeline_mode=pl.Buffered(1))],
        out_specs=[pl.BlockSpec((1, tm, QKV_W), lambda si, b: (b, si, 0)),
                   pl.BlockSpec((1, tm, GATES_W), lambda si, b: (b, si, 0))],
        out_shape=[jax.ShapeDtypeStruct((B, S, QKV_W), BF16),
                   jax.ShapeDtypeStruct((B, S, GATES_W), BF16)],
        compiler_params=pltpu.CompilerParams(vmem_limit_bytes=VMEM_LIMIT),
        name="in_proj",
    )(x, ada3, ln1_w.reshape(1, D), tabs, w_in_bf16)


def _moba_steps(q_ref, k_ref, v_ref, o_ref, vt_scr):
    S = q_ref.shape[1]
    T = MOBA_BLOCK
    nb = S // T
    lane = lax.broadcasted_iota(jnp.int32, (1, PAIR_W), 1)
    head_masks = [(lane >= h * HEAD_DIM) & (lane < (h + 1) * HEAD_DIM) for h in range(2)]
    krow = lax.broadcasted_iota(jnp.int32, (T, 2 * T), 0)
    qcol = lax.broadcasted_iota(jnp.int32, (T, 2 * T), 1) % T
    causal2 = krow <= qcol

    def score_steps(j, st):
        qt = q_ref[0, j * T:(j + 1) * T, :]
        zero = jnp.zeros_like(qt)
        q2 = jnp.concatenate([jnp.where(hm, qt, zero) for hm in head_masks], axis=0)
        st["blocks"], st["cmax"] = [], []
        for n in range(j + 1):
            def step(n=n):
                sn = lax.dot_general(k_ref[0, n * T:(n + 1) * T, :], q2, NT_DIMS,
                                     preferred_element_type=F32)
                if n == j:
                    sn = jnp.where(causal2, sn, -jnp.inf)
                sb = sn.astype(BF16)
                st["blocks"].append(sb)
                st["cmax"].append(jnp.max(sb, axis=0, keepdims=True).astype(F32))
            yield step

    biases = []

    def prologue_steps():
        for n in range(nb):
            vt = v_ref[0, n * T:(n + 1) * T, :].T
            for h in range(2):
                vt_scr[h, 0:HEAD_DIM, n * T:(n + 1) * T] = vt[h * HEAD_DIM:(h + 1) * HEAD_DIM]
                vt_scr[h, HEAD_DIM:, n * T:(n + 1) * T] = jnp.ones((SUM_ROWS, T), BF16)
            if n % 2:
                yield
        kmean = jnp.mean(k_ref[0].astype(F32).reshape(nb, T, PAIR_W), axis=1)
        km_rows = []
        for hm in head_masks:
            kmh = jnp.where(hm, kmean, 0.0)
            hi = kmh.astype(BF16).astype(F32)
            mid = (kmh - hi).astype(BF16).astype(F32)
            km_rows += [hi, mid, kmh - hi - mid]
        km = jnp.concatenate(km_rows, axis=0).astype(BF16)
        gate_terms = lax.dot_general(km, q_ref[0], NT_DIMS,
                                     preferred_element_type=F32)
        yield
        blk = lax.broadcasted_iota(jnp.int32, (nb, S), 0)
        qblk = lax.broadcasted_iota(jnp.int32, (nb, S), 1) // T
        past = blk < qblk
        for h in range(2):
            gt = gate_terms[3 * nb * h:3 * nb * (h + 1)]
            gate = gt[0:nb] + gt[nb:2 * nb] + gt[2 * nb:3 * nb]
            g = jnp.where(past, gate, -jnp.inf)
            rank = jnp.zeros((nb, S), jnp.int32)
            for m in range(nb - 1):
                gm = g[m:m + 1, :]
                beats = (gm > g) | ((gm == g) & (m < blk))
                rank = rank + beats.astype(jnp.int32)
            sel = past & (rank < MOBA_TOPK)
            biases.append(jnp.where(sel, 0.0, -jnp.inf).astype(F32))
            yield

    order = list(range(nb - 1, -1, -1))
    states = [dict() for _ in range(nb)]
    first = [step for j in order[:SCORE_LOOKAHEAD] for step in score_steps(j, states[j])]
    pro = prologue_steps()
    for step in first:
        step()
        next(pro, None)
        yield
    for _ in pro:
        yield

    def value_steps(j, st):
        bias = jnp.concatenate([b[:, j * T:(j + 1) * T] for b in biases], axis=1)
        m = st["cmax"][j]
        for n in range(j):
            m = jnp.maximum(m, st["cmax"][n] + bias[n:n + 1])
        probs = []
        for n in range(j + 1):
            def step(n=n):
                off = m if n == j else m - bias[n:n + 1]
                probs.append(jnp.exp2(st["blocks"][n] - off.astype(BF16)))
            yield step

        def finish():
            p_all = jnp.concatenate(probs, axis=0)
            acc = [jnp.dot(vt_scr[h, :, 0:(j + 1) * T], p_all[:, h * T:(h + 1) * T],
                           preferred_element_type=F32) for h in range(2)]
            outs = [a[0:HEAD_DIM] * (1.0 / a[HEAD_DIM:HEAD_DIM + 1]) for a in acc]
            o_ref[0, j * T:(j + 1) * T, :] = jnp.concatenate(outs, axis=0).T.astype(BF16)
        yield finish

    for i, j in enumerate(order):
        ahead = i + SCORE_LOOKAHEAD
        nxt = list(score_steps(order[ahead], states[order[ahead]])) if ahead < nb else []
        cur = list(value_steps(j, states[j]))
        for t in range(max(len(nxt), len(cur))):
            if t < len(nxt):
                nxt[t]()
            if t < len(cur):
                cur[t]()
            yield
        states[j].clear()


def _ret_steps(q_ref, k_ref, v_ref, g_ref, gn_ref, dec_ref, zeta_ref, xi_ref, cdec_ref, o_ref):
    S = q_ref.shape[1]
    C = RET_CHUNK
    nc = S // C
    lane = lax.broadcasted_iota(jnp.int32, (1, PAIR_W), 1)
    state = [None, None]
    gains = [gn_ref[:, h * RET_V_DIM:(h + 1) * RET_V_DIM] * math.sqrt(RET_V_DIM) for h in range(2)]
    for n in range(nc):
        rows = slice(n * C, (n + 1) * C)
        q = q_ref[0, rows, :]
        kc = k_ref[0, rows, :]
        kt = kc.T
        for h in range(2):
            hm = (lane >= h * HEAD_DIM) & (lane < (h + 1) * HEAD_DIM)
            cols = slice(h * RET_V_DIM, (h + 1) * RET_V_DIM)
            qh = jnp.where(hm, q, jnp.zeros_like(q))
            vh = v_ref[0, rows, cols]
            s = lax.dot_general(qh, kc, NT_DIMS, preferred_element_type=F32)
            p = s.astype(BF16) * dec_ref[h]
            o = jnp.dot(p, vh, preferred_element_type=F32)
            if n > 0:
                o = o + jnp.dot(qh, state[h].astype(BF16), preferred_element_type=F32) * xi_ref[h]
            if n < nc - 1:
                kz = kt * zeta_ref[h]
                kv = jnp.dot(kz, vh, preferred_element_type=F32)
                state[h] = kv if n == 0 else state[h] * cdec_ref[h] + kv
            yield
            d = o - jnp.sum(o, axis=-1, keepdims=True) * (1.0 / RET_V_DIM)
            r = lax.rsqrt(jnp.sum(d * d, axis=-1, keepdims=True) + RET_V_DIM * EPS)
            y = d * r * gains[h]
            hg = 0.5 * g_ref[0, rows, cols].astype(F32)
            o_ref[0, rows, cols] = (hg * (1.0 + jnp.tanh(hg)) * y).astype(BF16)
            yield


def _moba_kernel(*refs):
    for _ in _moba_steps(*refs):
        pass


def _ret_kernel(*refs):
    for _ in _ret_steps(*refs):
        pass


def _ret_tables():
    C = RET_CHUNK
    log_g = jnp.log(1.0 - jnp.power(2.0, -5.0 - jnp.arange(HEADS, dtype=F32)))
    i = jnp.arange(C, dtype=F32)
    diff = i[:, None] - i[None, :]
    dec = jnp.where(diff >= 0, jnp.exp(jnp.maximum(diff, 0.0)[None] * log_g[:, None, None]), 0.0)
    zeta = jnp.exp((C - 1 - i)[None, :] * log_g[:, None])[:, None, :]
    xi = jnp.exp((i + 1)[None, :] * log_g[:, None])
    xi = jnp.broadcast_to(xi[:, :, None], (HEADS, C, RET_V_DIM))
    cdec = jnp.broadcast_to(jnp.exp(C * log_g)[:, None, None], (HEADS, 1, RET_V_DIM))
    return dec.astype(BF16), zeta.astype(BF16), xi, cdec


def _pair_block(col_off):
    return lambda b, p: (b, 0, col_off // PAIR_W + p)


def _moba(qkv):
    B, S, _ = qkv.shape
    nb = S // MOBA_BLOCK
    return pl.pallas_call(
        _moba_kernel,
        grid=(B, N_PAIRS),
        in_specs=[pl.BlockSpec((1, S, PAIR_W), _pair_block(0)),
                  pl.BlockSpec((1, S, PAIR_W), _pair_block(MOBA_W)),
                  pl.BlockSpec((1, S, PAIR_W), _pair_block(2 * MOBA_W))],
        out_specs=pl.BlockSpec((1, S, PAIR_W), _pair_block(0)),
        out_shape=jax.ShapeDtypeStruct((B, S, MOBA_W), BF16),
        scratch_shapes=[pltpu.VMEM((2, HEAD_DIM + SUM_ROWS, S), BF16)],
        name="moba",
    )(qkv, qkv, qkv)


def _ret(qkv, gates, gn_w):
    B, S, _ = qkv.shape
    C = RET_CHUNK
    v_blk = 5 * MOBA_W // (2 * RET_V_DIM)
    dec, zeta, xi, cdec = _ret_tables()
    return pl.pallas_call(
        _ret_kernel,
        grid=(B, N_PAIRS),
        in_specs=[pl.BlockSpec((1, S, PAIR_W), _pair_block(3 * MOBA_W)),
                  pl.BlockSpec((1, S, PAIR_W), _pair_block(4 * MOBA_W)),
                  pl.BlockSpec((1, S, 2 * RET_V_DIM), lambda b, p: (b, 0, v_blk + p)),
                  pl.BlockSpec((1, S, 2 * RET_V_DIM), lambda b, p: (b, 0, p)),
                  pl.BlockSpec((1, 2 * RET_V_DIM), lambda b, p: (0, p)),
                  pl.BlockSpec((2, C, C), lambda b, p: (p, 0, 0)),
                  pl.BlockSpec((2, 1, C), lambda b, p: (p, 0, 0)),
                  pl.BlockSpec((2, C, RET_V_DIM), lambda b, p: (p, 0, 0)),
                  pl.BlockSpec((2, 1, RET_V_DIM), lambda b, p: (p, 0, 0))],
        out_specs=pl.BlockSpec((1, S, 2 * RET_V_DIM), lambda b, p: (b, 0, p)),
        out_shape=jax.ShapeDtypeStruct((B, S, RET_V_W), BF16),
        name="ret",
    )(qkv, qkv, qkv, gates, gn_w.reshape(1, RET_V_W), dec, zeta, xi, cdec)


def _out_mlp_kernel(x_ref, a_ref, r_ref, ga_ref, gr_ref, ada_ref, ln2_ref, fw_ref,
                    wmo_ref, wro_ref, wout_ref, w1_ref, w2_ref, o_ref):
    gate1 = ada_ref[0, 2:3, :]
    shift2 = ada_ref[0, 3:4, :]
    scale2 = ada_ref[0, 4:5, :]
    gate2 = ada_ref[0, 5:6, :]

    def stages(rows):
        ya = jnp.dot(a_ref[0, rows, :], wmo_ref[...], preferred_element_type=F32)
        yr = jnp.dot(r_ref[0, rows, :], wro_ref[...], preferred_element_type=F32)
        yield
        merged = (_sigmoid(ga_ref[0, rows, :].astype(F32)) * ya
                  + _sigmoid(gr_ref[0, rows, :].astype(F32)) * yr).astype(BF16)
        yield
        mo = jnp.dot(merged, wout_ref[...], preferred_element_type=F32)
        yield
        x1 = x_ref[0, rows, :] + gate1 * mo
        h2 = (_rms_norm(x1, ln2_ref[...]) * (1.0 + scale2) + shift2).astype(BF16)
        yield
        acc = jnp.zeros_like(x1)
        for ci in range(D_FF // FF_CHUNK):
            c0 = ci * FF_CHUNK
            hid = jnp.dot(h2, w1_ref[:, c0:c0 + FF_CHUNK], preferred_element_type=F32)
            yield
            hid = jnp.square(jnp.maximum(hid, 0.0)).astype(BF16)
            yield
            acc = acc + jnp.dot(hid, w2_ref[c0:c0 + FF_CHUNK, :], preferred_element_type=F32)
            yield
        x2 = x1 + gate2 * acc
        o_ref[0, rows, :] = _rms_norm(x2, fw_ref[...])

    _staggered(stages(rows) for rows in _row_groups(x_ref.shape[1]))


def _out_mlp(x, a_out, r_out, gates, ada3, ln2_w, final_w, wmo, wro, wout, w1, w2):
    B, S, D = x.shape
    tm = TOKEN_TILE
    const = functools.partial(pl.BlockSpec, pipeline_mode=pl.Buffered(1))
    return pl.pallas_call(
        _out_mlp_kernel,
        grid=(B, S // tm),
        in_specs=[pl.BlockSpec((1, tm, D), lambda b, si: (b, si, 0)),
                  pl.BlockSpec((1, tm, MOBA_W), lambda b, si: (b, si, 0)),
                  pl.BlockSpec((1, tm, RET_V_W), lambda b, si: (b, si, 0)),
                  pl.BlockSpec((1, tm, D), lambda b, si: (b, si, 1)),
                  pl.BlockSpec((1, tm, D), lambda b, si: (b, si, 2)),
                  pl.BlockSpec((1, N_ADA, D), lambda b, si: (b, 0, 0)),
                  pl.BlockSpec((1, D), lambda b, si: (0, 0)),
                  pl.BlockSpec((1, D), lambda b, si: (0, 0)),
                  const((MOBA_W, D), lambda b, si: (0, 0)),
                  const((RET_V_W, D), lambda b, si: (0, 0)),
                  const((D, D), lambda b, si: (0, 0)),
                  const((D, D_FF), lambda b, si: (0, 0)),
                  const((D_FF, D), lambda b, si: (0, 0))],
        out_specs=pl.BlockSpec((1, tm, D), lambda b, si: (b, si, 0)),
        out_shape=jax.ShapeDtypeStruct((B, S, D), F32),
        compiler_params=pltpu.CompilerParams(vmem_limit_bytes=VMEM_LIMIT),
        name="out_mlp",
    )(x, a_out, r_out, gates, gates, ada3, ln2_w.reshape(1, D), final_w.reshape(1, D),
      wmo, wro, wout, w1, w2)


def _rotary_tables(S, inv_freq, scale):
    lane = jnp.arange(PAIR_W)
    upper = (lane % HEAD_DIM) >= HEAD_DIM // 2
    pos = jnp.arange(S, dtype=F32)
    ang = pos[:, None] * inv_freq[lane % (HEAD_DIM // 2)][None, :]
    cos, sin = jnp.cos(ang), jnp.sin(ang)
    return jnp.stack([cos, jnp.where(upper, sin, 0.0), jnp.where(upper, 0.0, -sin)]) * scale


def kernel(x, c, ln1_w, ln2_w, w_ada, b_ada, w_in, ret_gn_w, w_moba_o, w_ret_o,
           w_out, w_ff1, w_ff2, final_norm_w):
    B, S, D = x.shape
    assert w_in.shape[0] == 1, "final norm is fused into the single layer's MLP kernel"
    moba_inv = 1.0 / (ROPE_THETA ** (jnp.arange(0, HEAD_DIM, 2, dtype=F32) / HEAD_DIM))
    ret_inv = 1.0 / (ROPE_THETA ** jnp.linspace(0.0, 1.0, HEAD_DIM // 2, dtype=F32))
    qk_scale = HEAD_DIM ** -0.5
    tabs = jnp.stack([_rotary_tables(S, moba_inv, qk_scale * LOG2E),
                      _rotary_tables(S, moba_inv, 1.0),
                      _rotary_tables(S, ret_inv, 1.0),
                      _rotary_tables(S, ret_inv, qk_scale)])

    ada3 = _ada(c, w_ada[0], b_ada[0]).reshape(B, N_ADA, D)
    qkv, gates = _in_proj(x, ada3, ln1_w[0], tabs, w_in[0].astype(BF16))
    a_out = _moba(qkv)
    r_out = _ret(qkv, gates, ret_gn_w[0])
    return _out_mlp(x, a_out, r_out, gates, ada3, ln2_w[0], final_norm_w,
                    w_moba_o[0].astype(BF16), w_ret_o[0].astype(BF16), w_out[0].astype(BF16),
                    w_ff1[0].astype(BF16), w_ff2[0].astype(BF16))
```
